```python
import jax, jax.numpy as jnp
from jax import lax
import numpy as np

D_MODEL = 2048
BATCH = 4
SEQ = 8192
DEPTH = 1

MIX_WIDTH = D_MODEL
LRU_WIDTH = MIX_WIDTH // 2
GMLP_WIDTH = MIX_WIDTH - LRU_WIDTH
IN_WIDTH = 2 * LRU_WIDTH + 2 * GMLP_WIDTH
LRU_HEADS = 8
LRU_HEAD_DIM = LRU_WIDTH // LRU_HEADS
CONV_WIDTH = 4
CONV_LEFT = 2
LRU_C = 8.0
GMLP_HEADS = 8
GMLP_HEAD_DIM = GMLP_WIDTH // GMLP_HEADS
CHUNK = 128
PEER_HEADS = 8
PEER_KEYS = 128
PEER_EXPERTS = PEER_KEYS * PEER_KEYS
PEER_TOPK = 16
PEER_QDIM = 256
PEER_HALF = PEER_QDIM // 2
PEER_BLOCK = 128
EPS = 1e-6

kernel_name = "hybrid_rglru_spatialgate_peer_encoder"


def rms_norm(x, g):
    xf = x.astype(jnp.float32)
    y = xf * lax.rsqrt(jnp.mean(xf * xf, axis=-1, keepdims=True) + EPS)
    return (y * g.astype(jnp.float32)).astype(x.dtype)


def layer_norm(x, g, b):
    xf = x.astype(jnp.float32)
    mu = jnp.mean(xf, axis=-1, keepdims=True)
    xc = xf - mu
    y = xc * lax.rsqrt(jnp.mean(xc * xc, axis=-1, keepdims=True) + EPS)
    return (y * g.astype(jnp.float32) + b.astype(jnp.float32)).astype(x.dtype)


def centred_depthwise_conv(x, w, b):
    S = x.shape[1]
    xp = jnp.pad(x, ((0, 0), (CONV_LEFT, CONV_WIDTH - 1 - CONV_LEFT), (0, 0)))
    y = b + xp[:, 0:S] * w[0]
    for k in range(1, CONV_WIDTH):
        y = y + xp[:, k:k + S] * w[k]
    return y


def _recurrence_combine(left, right):
    a_l, b_l = left
    a_r, b_r = right
    return a_l * a_r, a_r * b_l + b_r


def rglru(x, w_a, b_a, w_x, b_x, lam, reverse):
    B, S, _ = x.shape
    xh = x.reshape(B, S, LRU_HEADS, LRU_HEAD_DIM)
    r = jax.nn.sigmoid(jnp.einsum('bshc,hce->bshe', xh, w_a.astype(jnp.float32)).reshape(B, S, LRU_WIDTH) + b_a.astype(jnp.float32))
    i = jax.nn.sigmoid(jnp.einsum('bshc,hce->bshe', xh, w_x.astype(jnp.float32)).reshape(B, S, LRU_WIDTH) + b_x.astype(jnp.float32))
    log_a = -LRU_C * jax.nn.softplus(-lam.astype(jnp.float32)) * r
    a = jnp.exp(log_a)
    u = jnp.sqrt(-jnp.expm1(2.0 * log_a)) * (i * x)
    _, h = lax.associative_scan(_recurrence_combine, (a, u), reverse=reverse, axis=1)
    return h


def recurrent_group(xr, gr, conv_w, conv_b, w_a, b_a, w_x, b_x, lam):
    xc = centred_depthwise_conv(xr, conv_w, conv_b).astype(jnp.float32)
    h = (rglru(xc, w_a[0], b_a[0], w_x[0], b_x[0], lam[0], reverse=False)
         + rglru(xc, w_a[1], b_a[1], w_x[1], b_x[1], lam[1], reverse=True))
    return (jax.nn.gelu(gr.astype(jnp.float32)) * h).astype(xr.dtype)


def spatial_gating_group(gu, gv, ln_g, ln_b, w_s, b_s):
    B, S, _ = gv.shape
    u = jax.nn.gelu(gu)
    v = layer_norm(jax.nn.gelu(gv), ln_g, ln_b)
    vc = v.reshape(B, S // CHUNK, CHUNK, GMLP_HEADS, GMLP_HEAD_DIM)
    mixed = jnp.einsum('hpq,bnqhc->bnphc', w_s, vc) + b_s.T[None, None, :, :, None]
    return u * mixed.reshape(B, S, GMLP_WIDTH)


def peer(x, w_q, sub_keys, expert_u, expert_v):
    B, S, D = x.shape
    q = (x @ w_q).reshape(B, S, PEER_HEADS, 2, PEER_HALF)
    s = jnp.einsum('bshpk,hpnk->bshpn', q, sub_keys).astype(jnp.float32)
    top_s, top_i = lax.top_k(s, PEER_TOPK)
    cand_s = (top_s[..., 0, :, None] + top_s[..., 1, None, :]).reshape(B, S, PEER_HEADS, PEER_TOPK * PEER_TOPK)
    cand_i = (top_i[..., 0, :, None] * PEER_KEYS + top_i[..., 1, None, :]).reshape(B, S, PEER_HEADS, PEER_TOPK * PEER_TOPK)
    best_s, pos = lax.top_k(cand_s, PEER_TOPK)
    ids = jnp.take_along_axis(cand_i, pos, axis=-1)
    g = jax.nn.softmax(best_s, axis=-1).astype(x.dtype)
    n_blk = (B * S) // PEER_BLOCK
    xb = x.reshape(n_blk, PEER_BLOCK, D)
    ib = ids.reshape(n_blk, PEER_BLOCK, PEER_HEADS * PEER_TOPK)
    gb = g.reshape(n_blk, PEER_BLOCK, PEER_HEADS * PEER_TOPK)

    def block(args):
        xt, it, gt = args
        u = jnp.take(expert_u, it, axis=0)
        act = jax.nn.gelu(jnp.einsum('td,tkd->tk', xt, u)) * gt
        v = jnp.take(expert_v, it, axis=0)
        return jnp.einsum('tk,tkd->td', act, v)

    out = lax.map(block, (xb, ib, gb))
    return out.reshape(B, S, D)


def setup_inputs(seed: int = 0) -> dict:
    key = jax.random.key(seed)
    ks = jax.random.split(key, 24)
    f32 = jnp.float32
    nrm = lambda k, shape, scale: jax.random.normal(k, shape, f32) * scale
    lam_u = jax.random.uniform(ks[10], (DEPTH, 2, LRU_WIDTH), f32, minval=0.9, maxval=0.999)
    lam_base = lam_u ** (1.0 / LRU_C)
    return {
        "x": nrm(ks[0], (BATCH, SEQ, D_MODEL), 1.0),
        "mix_norm_g": 1.0 + nrm(ks[1], (DEPTH, D_MODEL), 0.02),
        "w_in": nrm(ks[2], (DEPTH, D_MODEL, IN_WIDTH), D_MODEL ** -0.5),
        "conv_w": nrm(ks[3], (DEPTH, CONV_WIDTH, LRU_WIDTH), CONV_WIDTH ** -0.5),
        "conv_b": nrm(ks[4], (DEPTH, LRU_WIDTH), 0.02),
        "lru_w_a": nrm(ks[5], (DEPTH, 2, LRU_HEADS, LRU_HEAD_DIM, LRU_HEAD_DIM), LRU_HEAD_DIM ** -0.5),
        "lru_b_a": nrm(ks[6], (DEPTH, 2, LRU_WIDTH), 0.02),
        "lru_w_x": nrm(ks[7], (DEPTH, 2, LRU_HEADS, LRU_HEAD_DIM, LRU_HEAD_DIM), LRU_HEAD_DIM ** -0.5),
        "lru_b_x": nrm(ks[8], (DEPTH, 2, LRU_WIDTH), 0.02),
        "lru_lambda": jnp.log(lam_base) - jnp.log1p(-lam_base),
        "gmlp_ln_g": 1.0 + nrm(ks[11], (DEPTH, GMLP_WIDTH), 0.02),
        "gmlp_ln_b": nrm(ks[12], (DEPTH, GMLP_WIDTH), 0.02),
        "gmlp_w_s": nrm(ks[13], (DEPTH, GMLP_HEADS, CHUNK, CHUNK), 0.5 * CHUNK ** -0.5),
        "gmlp_b_s": 1.0 + nrm(ks[14], (DEPTH, GMLP_HEADS, CHUNK), 0.1),
        "w_out": nrm(ks[15], (DEPTH, MIX_WIDTH, D_MODEL), MIX_WIDTH ** -0.5),
        "ffn_norm_g": 1.0 + nrm(ks[16], (DEPTH, D_MODEL), 0.02),
        "peer_w_q": nrm(ks[17], (DEPTH, D_MODEL, PEER_HEADS * PEER_QDIM), D_MODEL ** -0.5),
        "peer_sub_keys": nrm(ks[18], (DEPTH, PEER_HEADS, 2, PEER_KEYS, PEER_HALF), PEER_HALF ** -0.5),
        "peer_u": nrm(ks[19], (DEPTH, PEER_EXPERTS, D_MODEL), D_MODEL ** -0.5),
        "peer_v": nrm(ks[20], (DEPTH, PEER_EXPERTS, D_MODEL), PEER_HEADS ** -0.5),
        "final_norm_g": 1.0 + nrm(ks[21], (D_MODEL,), 0.02),
    }


def reference(x, mix_norm_g, w_in, conv_w, conv_b, lru_w_a, lru_b_a, lru_w_x, lru_b_x, lru_lambda,
              gmlp_ln_g, gmlp_ln_b, gmlp_w_s, gmlp_b_s, w_out, ffn_norm_g, peer_w_q, peer_sub_keys,
              peer_u, peer_v, final_norm_g):
    h = x
    for l in range(DEPTH):
        n = rms_norm(h, mix_norm_g[l])
        z = n @ w_in[l]
        xr = z[..., :LRU_WIDTH]
        gr = z[..., LRU_WIDTH:2 * LRU_WIDTH]
        gu = z[..., 2 * LRU_WIDTH:2 * LRU_WIDTH + GMLP_WIDTH]
        gv = z[..., 2 * LRU_WIDTH + GMLP_WIDTH:]
        a_out = recurrent_group(xr, gr, conv_w[l], conv_b[l], lru_w_a[l], lru_b_a[l],
                                lru_w_x[l], lru_b_x[l], lru_lambda[l])
        b_out = spatial_gating_group(gu, gv, gmlp_ln_g[l], gmlp_ln_b[l], gmlp_w_s[l], gmlp_b_s[l])
        h = h + jnp.concatenate([a_out, b_out], axis=-1) @ w_out[l]
        h = h + peer(rms_norm(h, ffn_norm_g[l]), peer_w_q[l], peer_sub_keys[l], peer_u[l], peer_v[l])
    return rms_norm(h, final_norm_g)
```

```python
import functools

import jax
import jax.numpy as jnp
from jax import lax
from jax.experimental import pallas as pl
from jax.experimental.pallas import tpu as pltpu

F32 = jnp.float32
BF16 = jnp.bfloat16

EPS = 1e-6
LRU_C = 8.0
LRU_HEADS = 8
CONV_WIDTH = 4
CONV_LEFT = 2
GMLP_HEADS = 8
CHUNK = 128
PEER_HEADS = 8
PEER_KEYS = 128
PEER_TOPK = 16

LANES = 128
SUBLANES = 8
VMEM_LIMIT = 56 * 1024 * 1024

_NT = (((1,), (1,)), ((), ()))


def _gelu(x):
    c = 0.7978845608028654
    return 0.5 * x * (1.0 + jnp.tanh(c * (x + 0.044715 * (x * x * x))))


def _sigmoid(x):
    return 1.0 / (1.0 + jnp.exp(-x))


def _params(sem):
    return pltpu.CompilerParams(dimension_semantics=sem, vmem_limit_bytes=VMEM_LIMIT)


def _mix_in_kernel(x_ref, g_ref, w_ref, o_ref, n_scr):
    j = pl.program_id(1)

    @pl.when(j == 0)
    def _():
        x = x_ref[...]
        ms = jnp.mean(x * x, axis=-1, keepdims=True)
        n_scr[...] = (x * lax.rsqrt(ms + EPS) * g_ref[...]).astype(BF16)

    acc = jnp.dot(n_scr[...], w_ref[...], preferred_element_type=F32)

    @pl.when(j == 0)
    def _():
        o_ref[...] = acc

    @pl.when(j > 0)
    def _():
        o_ref[...] = _gelu(acc)


def _mix_in(x2, g, w_bf, tm):
    M, D = x2.shape
    N = w_bf.shape[1]
    tn = N // 4
    return pl.pallas_call(
        _mix_in_kernel,
        grid=(M // tm, 4),
        in_specs=[
            pl.BlockSpec((tm, D), lambda i, j: (i, 0)),
            pl.BlockSpec((1, D), lambda i, j: (0, 0)),
            pl.BlockSpec((D, tn), lambda i, j: (0, j)),
        ],
        out_specs=pl.BlockSpec((tm, tn), lambda i, j: (i, j)),
        out_shape=jax.ShapeDtypeStruct((M, N), F32),
        scratch_shapes=[pltpu.VMEM((tm, D), BF16)],
        compiler_params=_params(("parallel", "arbitrary")),
    )(x2, g.reshape(1, D), w_bf)


def _lru_kernel(xf_ref, xfp_ref, xfn_ref, xb_ref, xbp_ref, xbn_ref,
                cw_ref, cb_ref, wa_ref, ba_ref, wx_ref, bx_ref, lam_ref,
                hf_ref, hb_ref,
                xe_scr, a_scr, u_scr, cf_scr, cr_scr, *, tt, nt):
    j = pl.program_id(1)
    C = xf_ref.shape[-1]
    hd = C // LRU_HEADS

    @pl.when(j == 0)
    def _():
        cf_scr[...] = jnp.zeros_like(cf_scr)
        cr_scr[...] = jnp.zeros_like(cr_scr)

    def gates(cur_ref, prev_ref, next_ref, jt, d):
        xe_scr[0:SUBLANES, :] = jnp.where(jt > 0, prev_ref[0], 0.0)
        xe_scr[SUBLANES:SUBLANES + tt, :] = cur_ref[0]
        xe_scr[SUBLANES + tt:2 * SUBLANES + tt, :] = jnp.where(jt < nt - 1, next_ref[0], 0.0)
        xc = cb_ref[...] + jnp.zeros((tt, C), F32)
        for k in range(CONV_WIDTH):
            off = SUBLANES + k - CONV_LEFT
            xc = xc + xe_scr[off:off + tt, :] * cw_ref[k:k + 1, :]
        xcb = xc.astype(BF16)
        ra = jnp.concatenate(
            [jnp.dot(xcb[:, h * hd:(h + 1) * hd], wa_ref[d, h], preferred_element_type=F32)
             for h in range(LRU_HEADS)], axis=1) + ba_ref[d:d + 1, :]
        rx = jnp.concatenate(
            [jnp.dot(xcb[:, h * hd:(h + 1) * hd], wx_ref[d, h], preferred_element_type=F32)
             for h in range(LRU_HEADS)], axis=1) + bx_ref[d:d + 1, :]
        r = _sigmoid(ra)
        i = _sigmoid(rx)
        nl = -lam_ref[d:d + 1, :]
        sp = jnp.maximum(nl, 0.0) + jnp.log(1.0 + jnp.exp(-jnp.abs(nl)))
        a = jnp.exp(-LRU_C * sp * r)
        a_scr[...] = a
        u_scr[...] = jnp.sqrt(1.0 - a * a) * (i * xc)

    row = lax.broadcasted_iota(jnp.int32, (SUBLANES, C), 0)
    ng = tt // SUBLANES

    gates(xf_ref, xfp_ref, xfn_ref, j, 0)

    def fwd_body(g, carry):
        r0 = pl.multiple_of(g * SUBLANES, SUBLANES)
        a = a_scr[pl.ds(r0, SUBLANES), :]
        u = u_scr[pl.ds(r0, SUBLANES), :]
        for s in (1, 2, 4):
            m = row >= s
            u = u + a * jnp.where(m, pltpu.roll(u, s, 0), 0.0)
            a = a * jnp.where(m, pltpu.roll(a, s, 0), 1.0)
        h = u + a * carry
        hf_ref[0, pl.ds(r0, SUBLANES), :] = h
        return h[SUBLANES - 1:SUBLANES, :]

    cf_scr[...] = lax.fori_loop(0, ng, fwd_body, cf_scr[...])

    gates(xb_ref, xbp_ref, xbn_ref, nt - 1 - j, 1)

    def bwd_body(gi, carry):
        g = ng - 1 - gi
        r0 = pl.multiple_of(g * SUBLANES, SUBLANES)
        a = a_scr[pl.ds(r0, SUBLANES), :]
        u = u_scr[pl.ds(r0, SUBLANES), :]
        for s in (1, 2, 4):
            m = row < SUBLANES - s
            u = u + a * jnp.where(m, pltpu.roll(u, SUBLANES - s, 0), 0.0)
            a = a * jnp.where(m, pltpu.roll(a, SUBLANES - s, 0), 1.0)
        h = u + a * carry
        hb_ref[0, pl.ds(r0, SUBLANES), :] = h
        return h[0:1, :]

    cr_scr[...] = lax.fori_loop(0, ng, bwd_body, cr_scr[...])


def _lru(z3, conv_w, conv_b, w_a_bf, b_a, w_x_bf, b_x, lam, tt):
    B, S, _ = z3.shape
    C = conv_w.shape[-1]
    nt = S // tt
    tb = tt // SUBLANES
    nb8 = S // SUBLANES

    def cur(jt):
        return lambda b, j: (b, jt(j), 0)

    def prev(jt):
        return lambda b, j: (b, jnp.maximum(jt(j) * tb - 1, 0), 0)

    def nxt(jt):
        return lambda b, j: (b, jnp.minimum((jt(j) + 1) * tb, nb8 - 1), 0)

    fw = lambda j: j
    bw = lambda j: nt - 1 - j
    full = lambda shape: pl.BlockSpec(shape, lambda b, j: (0,) * len(shape))
    return pl.pallas_call(
        functools.partial(_lru_kernel, tt=tt, nt=nt),
        grid=(B, nt),
        in_specs=[
            pl.BlockSpec((1, tt, C), cur(fw)),
            pl.BlockSpec((1, SUBLANES, C), prev(fw)),
            pl.BlockSpec((1, SUBLANES, C), nxt(fw)),
            pl.BlockSpec((1, tt, C), cur(bw)),
            pl.BlockSpec((1, SUBLANES, C), prev(bw)),
            pl.BlockSpec((1, SUBLANES, C), nxt(bw)),
            full((CONV_WIDTH, C)), full((1, C)),
            full(w_a_bf.shape), full((2, C)), full(w_x_bf.shape), full((2, C)), full((2, C)),
        ],
        out_specs=[
            pl.BlockSpec((1, tt, C), cur(fw)),
            pl.BlockSpec((1, tt, C), cur(bw)),
        ],
        out_shape=[jax.ShapeDtypeStruct((B, S, C), F32), jax.ShapeDtypeStruct((B, S, C), F32)],
        scratch_shapes=[
            pltpu.VMEM((tt + 2 * SUBLANES, C), F32),
            pltpu.VMEM((tt, C), F32),
            pltpu.VMEM((tt, C), F32),
            pltpu.VMEM((1, C), F32),
            pltpu.VMEM((1, C), F32),
        ],
        compiler_params=_params(("parallel", "arbitrary")),
    )(z3, z3, z3, z3, z3, z3, conv_w, conv_b.reshape(1, C), w_a_bf, b_a, w_x_bf, b_x, lam)


def _gmlp_kernel(u_ref, v_ref, lng_ref, lnb_ref, ws_ref, bs_ref, o_ref, *, tc):
    v = v_ref[...]
    mu = jnp.mean(v, axis=-1, keepdims=True)
    vc = v - mu
    var = jnp.mean(vc * vc, axis=-1, keepdims=True)
    vb = (vc * lax.rsqrt(var + EPS) * lng_ref[...] + lnb_ref[...]).astype(BF16)
    hd = v.shape[-1] // GMLP_HEADS
    for c in range(tc // CHUNK):
        rows = slice(c * CHUNK, (c + 1) * CHUNK)
        mixed = jnp.concatenate(
            [jnp.dot(ws_ref[h], vb[rows, h * hd:(h + 1) * hd], preferred_element_type=F32)
             for h in range(GMLP_HEADS)], axis=1)
        o_ref[rows, :] = (u_ref[rows, :] * (mixed + bs_ref[...])).astype(BF16)


def _gmlp(z4, ln_g, ln_b, w_s_bf, bs_full, tc):
    M = z4.shape[0]
    C = ln_g.shape[-1]
    return pl.pallas_call(
        functools.partial(_gmlp_kernel, tc=tc),
        grid=(M // tc,),
        in_specs=[
            pl.BlockSpec((tc, C), lambda i: (i, 2)),
            pl.BlockSpec((tc, C), lambda i: (i, 3)),
            pl.BlockSpec((1, C), lambda i: (0, 0)),
            pl.BlockSpec((1, C), lambda i: (0, 0)),
            pl.BlockSpec(w_s_bf.shape, lambda i: (0, 0, 0)),
            pl.BlockSpec((CHUNK, C), lambda i: (0, 0)),
        ],
        out_specs=pl.BlockSpec((tc, C), lambda i: (i, 0)),
        out_shape=jax.ShapeDtypeStruct((M, C), BF16),
        compiler_params=_params(("parallel",)),
    )(z4, z4, ln_g.reshape(1, C), ln_b.reshape(1, C), w_s_bf, bs_full)


def _out_proj_kernel(hf_ref, hb_ref, gr_ref, bo_ref, x_ref, w_ref, g_ref, h1_ref, n2_ref):
    ca = hf_ref.shape[-1]
    a_out = ((hf_ref[...] + hb_ref[...]) * gr_ref[...]).astype(BF16)
    acc = jnp.dot(a_out, w_ref[0:ca, :], preferred_element_type=F32)
    acc = acc + jnp.dot(bo_ref[...], w_ref[ca:, :], preferred_element_type=F32)
    h1 = x_ref[...] + acc
    h1_ref[...] = h1
    ms = jnp.mean(h1 * h1, axis=-1, keepdims=True)
    n2_ref[...] = (h1 * lax.rsqrt(ms + EPS) * g_ref[...]).astype(BF16)


def _out_proj(hf, hb, z4, b_out, x2, w_bf, g, tm):
    M, D = x2.shape
    ca = hf.shape[-1]
    cb = b_out.shape[-1]
    return pl.pallas_call(
        _out_proj_kernel,
        grid=(M // tm,),
        in_specs=[
            pl.BlockSpec((tm, ca), lambda i: (i, 0)),
            pl.BlockSpec((tm, ca), lambda i: (i, 0)),
            pl.BlockSpec((tm, ca), lambda i: (i, 1)),
            pl.BlockSpec((tm, cb), lambda i: (i, 0)),
            pl.BlockSpec((tm, D), lambda i: (i, 0)),
            pl.BlockSpec(w_bf.shape, lambda i: (0, 0)),
            pl.BlockSpec((1, D), lambda i: (0, 0)),
        ],
        out_specs=[pl.BlockSpec((tm, D), lambda i: (i, 0)), pl.BlockSpec((tm, D), lambda i: (i, 0))],
        out_shape=[jax.ShapeDtypeStruct((M, D), F32), jax.ShapeDtypeStruct((M, D), BF16)],
        compiler_params=_params(("parallel",)),
    )(hf, hb, z4, b_out, x2, w_bf, g.reshape(1, D))


ROUTE_T = LANES
PLANE = ROUTE_T + SUBLANES
N_CAND = 64


def _route_kernel(n_ref, wq_ref, keys_ref, g_ref, q_scr, w_scr, i_scr, j_scr, wt_scr, it_scr, jt_scr, gs_scr):
    T = ROUTE_T
    K = PEER_TOPK
    NK = PEER_KEYS
    neg = -jnp.inf
    q_scr[...] = jnp.dot(n_ref[...], wq_ref[...], preferred_element_type=F32).astype(BF16)

    row_k = lax.broadcasted_iota(jnp.int32, (NK, T), 0).astype(F32)
    row_c = lax.broadcasted_iota(jnp.int32, (N_CAND, T), 0).astype(F32)
    row16 = lax.broadcasted_iota(jnp.int32, (K, T), 0)
    row8 = lax.broadcasted_iota(jnp.int32, (SUBLANES, T), 0)

    def top16(s):
        vals = jnp.zeros((K, T), F32)
        idxs = jnp.zeros((K, T), F32)
        for k in range(K):
            m = jnp.max(s, axis=0, keepdims=True)
            idx = jnp.min(jnp.where(s == m, row_k, float(NK)), axis=0, keepdims=True)
            vals = jnp.where(row16 == k, m, vals)
            idxs = jnp.where(row16 == k, idx, idxs)
            s = jnp.where(row_k == idx, neg, s)
        return vals, idxs

    def bc(x, p):
        return jnp.broadcast_to(x[p:p + 1, :], (SUBLANES, T))

    def pair_rows(a, b):
        lo = b[0:SUBLANES]
        lo4 = jnp.where(row8 < 4, lo, pltpu.roll(lo, 4, 0))
        return [
            (bc(a, 0), lo), (bc(a, 0), b[SUBLANES:2 * SUBLANES]),
            (bc(a, 1), lo), (bc(a, 2), lo), (bc(a, 3), lo),
            (jnp.where(row8 < 4, bc(a, 4), bc(a, 5)), lo4),
            (jnp.where(row8 < 4, bc(a, 6), bc(a, 7)), lo4),
            (a[SUBLANES:2 * SUBLANES], bc(b, 0)),
        ]

    def head_body(h, _):
        c0 = pl.multiple_of(h * 2 * NK, 2 * NK)
        qa = q_scr[:, pl.ds(c0, NK)]
        qb = q_scr[:, pl.ds(pl.multiple_of(c0 + NK, NK), NK)]
        sa = lax.dot_general(keys_ref[2 * h], qa, _NT, preferred_element_type=F32)
        sb = lax.dot_general(keys_ref[2 * h + 1], qb, _NT, preferred_element_type=F32)
        va, ia = top16(sa)
        vb, ib = top16(sb)
        vpairs = pair_rows(va, vb)
        ipairs = pair_rows(ia, ib)
        cand = jnp.concatenate([x + y for x, y in vpairs], axis=0)
        ci = jnp.concatenate([x for x, _ in ipairs], axis=0)
        cj = jnp.concatenate([y for _, y in ipairs], axis=0)
        best = jnp.zeros((K, T), F32)
        bi = jnp.zeros((K, T), F32)
        bj = jnp.zeros((K, T), F32)
        for k in range(K):
            m = jnp.max(cand, axis=0, keepdims=True)
            pos = jnp.min(jnp.where(cand == m, row_c, float(N_CAND)), axis=0, keepdims=True)
            sel = row_c == pos
            best = jnp.where(row16 == k, m, best)
            bi = jnp.where(row16 == k, jnp.sum(jnp.where(sel, ci, 0.0), axis=0, keepdims=True), bi)
            bj = jnp.where(row16 == k, jnp.sum(jnp.where(sel, cj, 0.0), axis=0, keepdims=True), bj)
            cand = jnp.where(sel, neg, cand)
        e = jnp.exp(best - best[0:1, :])
        w = e / jnp.sum(e, axis=0, keepdims=True)
        r0 = pl.multiple_of(h * K, K)
        w_scr[pl.ds(r0, K), :] = w
        i_scr[pl.ds(r0, K), :] = bi
        j_scr[pl.ds(r0, K), :] = bj
        return 0

    lax.fori_loop(0, PEER_HEADS, head_body, 0)

    wt_scr[...] = w_scr[...].T
    it_scr[...] = i_scr[...].T
    jt_scr[...] = j_scr[...].T

    sub = lax.broadcasted_iota(jnp.int32, (NK, PEER_HEADS * K), 0).astype(F32)

    def tok_body(t, _):
        wrow = wt_scr[pl.ds(t, 1), :]
        irow = it_scr[pl.ds(t, 1), :]
        jrow = jt_scr[pl.ds(t, 1), :]
        p1 = jnp.where(sub == irow, wrow, 0.0).astype(BF16)
        p2 = jnp.where(sub == jrow, 1.0, 0.0).astype(BF16)
        g = lax.dot_general(p1, p2, _NT, preferred_element_type=F32)
        for blk in range(NK // SUBLANES):
            gs_scr[pl.ds(blk * SUBLANES * PLANE + t, SUBLANES, stride=PLANE), :] = (
                g[blk * SUBLANES:(blk + 1) * SUBLANES, :])
        return 0

    lax.fori_loop(0, T, tok_body, 0)

    for i in range(NK):
        g_ref[:, i * NK:(i + 1) * NK] = gs_scr[i * PLANE:i * PLANE + T, :].astype(BF16)


def _route(n2, wq_bf, keys_bf):
    M, D = n2.shape
    Q = wq_bf.shape[1]
    T = ROUTE_T
    NK = PEER_KEYS
    HK = PEER_HEADS * PEER_TOPK
    return pl.pallas_call(
        _route_kernel,
        grid=(M // T,),
        in_specs=[
            pl.BlockSpec((T, D), lambda i: (i, 0)),
            pl.BlockSpec((D, Q), lambda i: (0, 0)),
            pl.BlockSpec(keys_bf.shape, lambda i: (0, 0, 0)),
        ],
        out_specs=pl.BlockSpec((T, NK * NK), lambda i: (i, 0)),
        out_shape=jax.ShapeDtypeStruct((M, NK * NK), BF16),
        scratch_shapes=[
            pltpu.VMEM((T, Q), BF16),
            pltpu.VMEM((HK, T), F32), pltpu.VMEM((HK, T), F32), pltpu.VMEM((HK, T), F32),
            pltpu.VMEM((T, HK), F32), pltpu.VMEM((T, HK), F32), pltpu.VMEM((T, HK), F32),
            pltpu.VMEM((NK * PLANE, NK), F32),
        ],
        compiler_params=_params(("parallel",)),
    )(n2, wq_bf, keys_bf)


def _expert_kernel(n_ref, u_ref, v_ref, g_ref, h_ref, fg_ref, o_ref, acc, *, final_norm):
    k = pl.program_id(1)

    @pl.when(k == 0)
    def _():
        acc[...] = jnp.zeros_like(acc)

    act = lax.dot_general(n_ref[...], u_ref[...], _NT, preferred_element_type=F32)
    a = (_gelu(act) * g_ref[...].astype(F32)).astype(BF16)
    acc[...] += jnp.dot(a, v_ref[...], preferred_element_type=F32)

    @pl.when(k == pl.num_programs(1) - 1)
    def _():
        h2 = h_ref[...] + acc[...]
        if final_norm:
            ms = jnp.mean(h2 * h2, axis=-1, keepdims=True)
            h2 = h2 * lax.rsqrt(ms + EPS) * fg_ref[...]
        o_ref[...] = h2


def _experts(n2, u_bf, v_bf, gates, h1, fg, tt, te, final_norm):
    M, D = n2.shape
    E = u_bf.shape[0]
    return pl.pallas_call(
        functools.partial(_expert_kernel, final_norm=final_norm),
        grid=(M // tt, E // te),
        in_specs=[
            pl.BlockSpec((tt, D), lambda i, k: (i, 0)),
            pl.BlockSpec((te, D), lambda i, k: (k, 0)),
            pl.BlockSpec((te, D), lambda i, k: (k, 0)),
            pl.BlockSpec((tt, te), lambda i, k: (i, k)),
            pl.BlockSpec((tt, D), lambda i, k: (i, 0)),
            pl.BlockSpec((1, D), lambda i, k: (0, 0)),
        ],
        out_specs=pl.BlockSpec((tt, D), lambda i, k: (i, 0)),
        out_shape=jax.ShapeDtypeStruct((M, D), F32),
        scratch_shapes=[pltpu.VMEM((tt, D), F32)],
        compiler_params=_params(("parallel", "arbitrary")),
    )(n2, u_bf, v_bf, gates, h1, fg.reshape(1, D))


def _tile(n, want):
    t = min(n, want)
    assert n % t == 0, (n, t)
    return t


def kernel(x, mix_norm_g, w_in, conv_w, conv_b, lru_w_a, lru_b_a, lru_w_x, lru_b_x, lru_lambda,
           gmlp_ln_g, gmlp_ln_b, gmlp_w_s, gmlp_b_s, w_out, ffn_norm_g, peer_w_q, peer_sub_keys,
           peer_u, peer_v, final_norm_g):
    B, S, D = x.shape
    M = B * S
    depth = w_in.shape[0]
    assert S % CHUNK == 0 and M % ROUTE_T == 0
    h = x.reshape(M, D)
    for l in range(depth):
        z4 = _mix_in(h, mix_norm_g[l], w_in[l].astype(BF16), _tile(M, 512))
        cw = z4.shape[1] // 4
        hf, hb = _lru(z4.reshape(B, S, 4 * cw), conv_w[l], conv_b[l],
                      lru_w_a[l].astype(BF16), lru_b_a[l], lru_w_x[l].astype(BF16), lru_b_x[l],
                      lru_lambda[l], _tile(S, 256))
        bs_full = jnp.repeat(gmlp_b_s[l].T, cw // GMLP_HEADS, axis=1)
        b_out = _gmlp(z4, gmlp_ln_g[l], gmlp_ln_b[l], gmlp_w_s[l].astype(BF16), bs_full, _tile(M, 512))
        h1, n2 = _out_proj(hf.reshape(M, cw), hb.reshape(M, cw), z4, b_out, h,
                           w_out[l].astype(BF16), ffn_norm_g[l], _tile(M, 256))
        keys = peer_sub_keys[l].reshape(PEER_HEADS * 2, PEER_KEYS, -1).astype(BF16)
        gates = _route(n2, peer_w_q[l].astype(BF16), keys)
        h = _experts(n2, peer_u[l].astype(BF16), peer_v[l].astype(BF16), gates, h1, final_norm_g,
                     _tile(M, 512), 512, l == depth - 1)
    return h.reshape(B, S, D)
```

```python
import functools

import jax
import jax.numpy as jnp
from jax import lax
from jax.experimental import pallas as pl
from jax.experimental.pallas import tpu as pltpu

F32 = jnp.float32
BF16 = jnp.bfloat16

EPS = 1e-6
LRU_C = 8.0
LRU_HEADS = 8
CONV_WIDTH = 4
CONV_LEFT = 2
GMLP_HEADS = 8
CHUNK = 128
PEER_HEADS = 8
PEER_KEYS = 128
PEER_TOPK = 16

LANES = 128
SUBLANES = 8
VMEM_LIMIT = 56 * 1024 * 1024

_NT = (((1,), (1,)), ((), ()))


def _gelu(x):
    c = 0.7978845608028654
    return 0.5 * x * (1.0 + jnp.tanh(c * (x + 0.044715 * (x * x * x))))


def _sigmoid(x):
    return 1.0 / (1.0 + jnp.exp(-x))


def _params(sem):
    return pltpu.CompilerParams(dimension_semantics=sem, vmem_limit_bytes=VMEM_LIMIT)


def _mix_in_kernel(x_ref, g_ref, w_ref, o_ref, n_scr):
    j = pl.program_id(1)

    @pl.when(j == 0)
    def _():
        x = x_ref[...]
        ms = jnp.mean(x * x, axis=-1, keepdims=True)
        n_scr[...] = (x * lax.rsqrt(ms + EPS) * g_ref[...]).astype(BF16)

    acc = jnp.dot(n_scr[...], w_ref[...], preferred_element_type=F32)

    @pl.when(j == 0)
    def _():
        o_ref[...] = acc

    @pl.when(j > 0)
    def _():
        o_ref[...] = _gelu(acc)


def _mix_in(x2, g, w_bf, tm):
    M, D = x2.shape
    N = w_bf.shape[1]
    tn = N // 4
    return pl.pallas_call(
        _mix_in_kernel,
        grid=(M // tm, 4),
        in_specs=[
            pl.BlockSpec((tm, D), lambda i, j: (i, 0)),
            pl.BlockSpec((1, D), lambda i, j: (0, 0)),
            pl.BlockSpec((D, tn), lambda i, j: (0, j)),
        ],
        out_specs=pl.BlockSpec((tm, tn), lambda i, j: (i, j)),
        out_shape=jax.ShapeDtypeStruct((M, N), F32),
        scratch_shapes=[pltpu.VMEM((tm, D), BF16)],
        compiler_params=_params(("parallel", "arbitrary")),
    )(x2, g.reshape(1, D), w_bf)


def _lru_kernel(xf_ref, xfp_ref, xfn_ref, xb_ref, xbp_ref, xbn_ref,
                cw_ref, cb_ref, wa_ref, ba_ref, wx_ref, bx_ref, lam_ref,
                hf_ref, hb_ref,
                xe_scr, a_scr, u_scr, cf_scr, cr_scr, *, tt, nt):
    j = pl.program_id(1)
    C = xf_ref.shape[-1]
    hd = C // LRU_HEADS

    @pl.when(j == 0)
    def _():
        cf_scr[...] = jnp.zeros_like(cf_scr)
        cr_scr[...] = jnp.zeros_like(cr_scr)

    def gates(cur_ref, prev_ref, next_ref, jt, d):
        xe_scr[0:SUBLANES, :] = jnp.where(jt > 0, prev_ref[0], 0.0)
        xe_scr[SUBLANES:SUBLANES + tt, :] = cur_ref[0]
        xe_scr[SUBLANES + tt:2 * SUBLANES + tt, :] = jnp.where(jt < nt - 1, next_ref[0], 0.0)
        xc = cb_ref[...] + jnp.zeros((tt, C), F32)
        for k in range(CONV_WIDTH):
            off = SUBLANES + k - CONV_LEFT
            xc = xc + xe_scr[off:off + tt, :] * cw_ref[k:k + 1, :]
        xcb = xc.astype(BF16)
        ra = jnp.concatenate(
            [jnp.dot(xcb[:, h * hd:(h + 1) * hd], wa_ref[d, h], preferred_element_type=F32)
             for h in range(LRU_HEADS)], axis=1) + ba_ref[d:d + 1, :]
        rx = jnp.concatenate(
            [jnp.dot(xcb[:, h * hd:(h + 1) * hd], wx_ref[d, h], preferred_element_type=F32)
             for h in range(LRU_HEADS)], axis=1) + bx_ref[d:d + 1, :]
        r = _sigmoid(ra)
        i = _sigmoid(rx)
        nl = -lam_ref[d:d + 1, :]
        sp = jnp.maximum(nl, 0.0) + jnp.log(1.0 + jnp.exp(-jnp.abs(nl)))
        a = jnp.exp(-LRU_C * sp * r)
        a_scr[...] = a
        u_scr[...] = jnp.sqrt(1.0 - a * a) * (i * xc)

    row = lax.broadcasted_iota(jnp.int32, (SUBLANES, C), 0)
    ng = tt // SUBLANES

    gates(xf_ref, xfp_ref, xfn_ref, j, 0)

    def fwd_body(g, carry):
        r0 = pl.multiple_of(g * SUBLANES, SUBLANES)
        a = a_scr[pl.ds(r0, SUBLANES), :]
        u = u_scr[pl.ds(r0, SUBLANES), :]
        for s in (1, 2, 4):
            m = row >= s
            u = u + a * jnp.where(m, pltpu.roll(u, s, 0), 0.0)
            a = a * jnp.where(m, pltpu.roll(a, s, 0), 1.0)
        h = u + a * carry
        hf_ref[0, pl.ds(r0, SUBLANES), :] = h
        return h[SUBLANES - 1:SUBLANES, :]

    cf_scr[...] = lax.fori_loop(0, ng, fwd_body, cf_scr[...])

    gates(xb_ref, xbp_ref, xbn_ref, nt - 1 - j, 1)

    def bwd_body(gi, carry):
        g = ng - 1 - gi
        r0 = pl.multiple_of(g * SUBLANES, SUBLANES)
        a = a_scr[pl.ds(r0, SUBLANES), :]
        u = u_scr[pl.ds(r0, SUBLANES), :]
        for s in (1, 2, 4):
            m = row < SUBLANES - s
            u = u + a * jnp.where(m, pltpu.roll(u, SUBLANES - s, 0), 0.0)
            a = a * jnp.where(m, pltpu.roll(a, SUBLANES - s, 0), 1.0)
        h = u + a * carry
        hb_ref[0, pl.ds(r0, SUBLANES), :] = h
        return h[0:1, :]

    cr_scr[...] = lax.fori_loop(0, ng, bwd_body, cr_scr[...])


def _lru(z3, conv_w, conv_b, w_a_bf, b_a, w_x_bf, b_x, lam, tt):
    B, S, _ = z3.shape
    C = conv_w.shape[-1]
    nt = S // tt
    tb = tt // SUBLANES
    nb8 = S // SUBLANES

    def cur(jt):
        return lambda b, j: (b, jt(j), 0)

    def prev(jt):
        return lambda b, j: (b, jnp.maximum(jt(j) * tb - 1, 0), 0)

    def nxt(jt):
        return lambda b, j: (b, jnp.minimum((jt(j) + 1) * tb, nb8 - 1), 0)

    fw = lambda j: j
    bw = lambda j: nt - 1 - j
    full = lambda shape: pl.BlockSpec(shape, lambda b, j: (0,) * len(shape))
    return pl.pallas_call(
        functools.partial(_lru_kernel, tt=tt, nt=nt),
        grid=(B, nt),
        in_specs=[
            pl.BlockSpec((1, tt, C), cur(fw)),
            pl.BlockSpec((1, SUBLANES, C), prev(fw)),
            pl.BlockSpec((1, SUBLANES, C), nxt(fw)),
            pl.BlockSpec((1, tt, C), cur(bw)),
            pl.BlockSpec((1, SUBLANES, C), prev(bw)),
            pl.BlockSpec((1, SUBLANES, C), nxt(bw)),
            full((CONV_WIDTH, C)), full((1, C)),
            full(w_a_bf.shape), full((2, C)), full(w_x_bf.shape), full((2, C)), full((2, C)),
        ],
        out_specs=[
            pl.BlockSpec((1, tt, C), cur(fw)),
            pl.BlockSpec((1, tt, C), cur(bw)),
        ],
        out_shape=[jax.ShapeDtypeStruct((B, S, C), F32), jax.ShapeDtypeStruct((B, S, C), F32)],
        scratch_shapes=[
            pltpu.VMEM((tt + 2 * SUBLANES, C), F32),
            pltpu.VMEM((tt, C), F32),
            pltpu.VMEM((tt, C), F32),
            pltpu.VMEM((1, C), F32),
            pltpu.VMEM((1, C), F32),
        ],
        compiler_params=_params(("parallel", "arbitrary")),
    )(z3, z3, z3, z3, z3, z3, conv_w, conv_b.reshape(1, C), w_a_bf, b_a, w_x_bf, b_x, lam)


def _gmlp_kernel(u_ref, v_ref, lng_ref, lnb_ref, ws_ref, bs_ref, o_ref, *, tc):
    v = v_ref[...]
    mu = jnp.mean(v, axis=-1, keepdims=True)
    vc = v - mu
    var = jnp.mean(vc * vc, axis=-1, keepdims=True)
    vb = (vc * lax.rsqrt(var + EPS) * lng_ref[...] + lnb_ref[...]).astype(BF16)
    hd = v.shape[-1] // GMLP_HEADS
    for c in range(tc // CHUNK):
        rows = slice(c * CHUNK, (c + 1) * CHUNK)
        mixed = jnp.concatenate(
            [jnp.dot(ws_ref[h], vb[rows, h * hd:(h + 1) * hd], preferred_element_type=F32)
             for h in range(GMLP_HEADS)], axis=1)
        o_ref[rows, :] = (u_ref[rows, :] * (mixed + bs_ref[...])).astype(BF16)


def _gmlp(z4, ln_g, ln_b, w_s_bf, bs_full, tc):
    M = z4.shape[0]
    C = ln_g.shape[-1]
    return pl.pallas_call(
        functools.partial(_gmlp_kernel, tc=tc),
        grid=(M // tc,),
        in_specs=[
            pl.BlockSpec((tc, C), lambda i: (i, 2)),
            pl.BlockSpec((tc, C), lambda i: (i, 3)),
            pl.BlockSpec((1, C), lambda i: (0, 0)),
            pl.BlockSpec((1, C), lambda i: (0, 0)),
            pl.BlockSpec(w_s_bf.shape, lambda i: (0, 0, 0)),
            pl.BlockSpec((CHUNK, C), lambda i: (0, 0)),
        ],
        out_specs=pl.BlockSpec((tc, C), lambda i: (i, 0)),
        out_shape=jax.ShapeDtypeStruct((M, C), BF16),
        compiler_params=_params(("parallel",)),
    )(z4, z4, ln_g.reshape(1, C), ln_b.reshape(1, C), w_s_bf, bs_full)


def _out_proj_kernel(hf_ref, hb_ref, gr_ref, bo_ref, x_ref, w_ref, g_ref, h1_ref, n2_ref):
    ca = hf_ref.shape[-1]
    a_out = ((hf_ref[...] + hb_ref[...]) * gr_ref[...]).astype(BF16)
    acc = jnp.dot(a_out, w_ref[0:ca, :], preferred_element_type=F32)
    acc = acc + jnp.dot(bo_ref[...], w_ref[ca:, :], preferred_element_type=F32)
    h1 = x_ref[...] + acc
    h1_ref[...] = h1
    ms = jnp.mean(h1 * h1, axis=-1, keepdims=True)
    n2_ref[...] = (h1 * lax.rsqrt(ms + EPS) * g_ref[...]).astype(BF16)


def _out_proj(hf, hb, z4, b_out, x2, w_bf, g, tm):
    M, D = x2.shape
    ca = hf.shape[-1]
    cb = b_out.shape[-1]
    return pl.pallas_call(
        _out_proj_kernel,
        grid=(M // tm,),
        in_specs=[
            pl.BlockSpec((tm, ca), lambda i: (i, 0)),
            pl.BlockSpec((tm, ca), lambda i: (i, 0)),
            pl.BlockSpec((tm, ca), lambda i: (i, 1)),
            pl.BlockSpec((tm, cb), lambda i: (i, 0)),
            pl.BlockSpec((tm, D), lambda i: (i, 0)),
            pl.BlockSpec(w_bf.shape, lambda i: (0, 0)),
            pl.BlockSpec((1, D), lambda i: (0, 0)),
        ],
        out_specs=[pl.BlockSpec((tm, D), lambda i: (i, 0)), pl.BlockSpec((tm, D), lambda i: (i, 0))],
        out_shape=[jax.ShapeDtypeStruct((M, D), F32), jax.ShapeDtypeStruct((M, D), BF16)],
        compiler_params=_params(("parallel",)),
    )(hf, hb, z4, b_out, x2, w_bf, g.reshape(1, D))


ROUTE_T = LANES
PLANE = ROUTE_T + SUBLANES
N_CAND = 64


def _route_kernel(n_ref, wq_ref, keys_ref, g_ref, q_scr, w_scr, i_scr, j_scr, wt_scr, it_scr, jt_scr, gs_scr):
    T = ROUTE_T
    K = PEER_TOPK
    NK = PEER_KEYS
    neg = -jnp.inf
    q_scr[...] = jnp.dot(n_ref[...], wq_ref[...], preferred_element_type=F32).astype(BF16)

    row_k = lax.broadcasted_iota(jnp.int32, (NK, T), 0).astype(F32)
    row_c = lax.broadcasted_iota(jnp.int32, (N_CAND, T), 0).astype(F32)
    row16 = lax.broadcasted_iota(jnp.int32, (K, T), 0)
    row8 = lax.broadcasted_iota(jnp.int32, (SUBLANES, T), 0)

    def top16(s):
        vals = jnp.zeros((K, T), F32)
        idxs = jnp.zeros((K, T), F32)
        for k in range(K):
            m = jnp.max(s, axis=0, keepdims=True)
            idx = jnp.min(jnp.where(s == m, row_k, float(NK)), axis=0, keepdims=True)
            vals = jnp.where(row16 == k, m, vals)
            idxs = jnp.where(row16 == k, idx, idxs)
            s = jnp.where(row_k == idx, neg, s)
        return vals, idxs

    def bc(x, p):
        return jnp.broadcast_to(x[p:p + 1, :], (SUBLANES, T))

    def pair_rows(a, b):
        lo = b[0:SUBLANES]
        lo4 = jnp.where(row8 < 4, lo, pltpu.roll(lo, 4, 0))
        return [
            (bc(a, 0), lo), (bc(a, 0), b[SUBLANES:2 * SUBLANES]),
            (bc(a, 1), lo), (bc(a, 2), lo), (bc(a, 3), lo),
            (jnp.where(row8 < 4, bc(a, 4), bc(a, 5)), lo4),
            (jnp.where(row8 < 4, bc(a, 6), bc(a, 7)), lo4),
            (a[SUBLANES:2 * SUBLANES], bc(b, 0)),
        ]

    def head_body(h, _):
        c0 = pl.multiple_of(h * 2 * NK, 2 * NK)
        qa = q_scr[:, pl.ds(c0, NK)]
        qb = q_scr[:, pl.ds(pl.multiple_of(c0 + NK, NK), NK)]
        sa = lax.dot_general(keys_ref[2 * h], qa, _NT, preferred_element_type=F32)
        sb = lax.dot_general(keys_ref[2 * h + 1], qb, _NT, preferred_element_type=F32)
        va, ia = top16(sa)
        vb, ib = top16(sb)
        vpairs = pair_rows(va, vb)
        ipairs = pair_rows(ia, ib)
        cand = jnp.concatenate([x + y for x, y in vpairs], axis=0)
        ci = jnp.concatenate([x for x, _ in ipairs], axis=0)
        cj = jnp.concatenate([y for _, y in ipairs], axis=0)
        best = jnp.zeros((K, T), F32)
        bi = jnp.zeros((K, T), F32)
        bj = jnp.zeros((K, T), F32)
        for k in range(K):
            m = jnp.max(cand, axis=0, keepdims=True)
            pos = jnp.min(jnp.where(cand == m, row_c, float(N_CAND)), axis=0, keepdims=True)
            sel = row_c == pos
            best = jnp.where(row16 == k, m, best)
            bi = jnp.where(row16 == k, jnp.sum(jnp.where(sel, ci, 0.0), axis=0, keepdims=True), bi)
            bj = jnp.where(row16 == k, jnp.sum(jnp.where(sel, cj, 0.0), axis=0, keepdims=True), bj)
            cand = jnp.where(sel, neg, cand)
        e = jnp.exp(best - best[0:1, :])
        w = e / jnp.sum(e, axis=0, keepdims=True)
        r0 = pl.multiple_of(h * K, K)
        w_scr[pl.ds(r0, K), :] = w
        i_scr[pl.ds(r0, K), :] = bi
        j_scr[pl.ds(r0, K), :] = bj
        return 0

    lax.fori_loop(0, PEER_HEADS, head_body, 0)

    wt_scr[...] = w_scr[...].T
    it_scr[...] = i_scr[...].T
    jt_scr[...] = j_scr[...].T

    sub = lax.broadcasted_iota(jnp.int32, (NK, PEER_HEADS * K), 0).astype(F32)

    def tok_body(t, _):
        wrow = wt_scr[pl.ds(t, 1), :]
        irow = it_scr[pl.ds(t, 1), :]
        jrow = jt_scr[pl.ds(t, 1), :]
        p1 = jnp.where(sub == irow, wrow, 0.0).astype(BF16)
        p2 = jnp.where(sub == jrow, 1.0, 0.0).astype(BF16)
        g = lax.dot_general(p1, p2, _NT, preferred_element_type=F32)
        for blk in range(NK // SUBLANES):
            gs_scr[pl.ds(blk * SUBLANES * PLANE + t, SUBLANES, stride=PLANE), :] = (
                g[blk * SUBLANES:(blk + 1) * SUBLANES, :])
        return 0

    lax.fori_loop(0, T, tok_body, 0, unroll=8)

    for i in range(NK):
        g_ref[:, i * NK:(i + 1) * NK] = gs_scr[i * PLANE:i * PLANE + T, :].astype(BF16)


def _route(n2, wq_bf, keys_bf):
    M, D = n2.shape
    Q = wq_bf.shape[1]
    T = ROUTE_T
    NK = PEER_KEYS
    HK = PEER_HEADS * PEER_TOPK
    return pl.pallas_call(
        _route_kernel,
        grid=(M // T,),
        in_specs=[
            pl.BlockSpec((T, D), lambda i: (i, 0)),
            pl.BlockSpec((D, Q), lambda i: (0, 0)),
            pl.BlockSpec(keys_bf.shape, lambda i: (0, 0, 0)),
        ],
        out_specs=pl.BlockSpec((T, NK * NK), lambda i: (i, 0)),
        out_shape=jax.ShapeDtypeStruct((M, NK * NK), BF16),
        scratch_shapes=[
            pltpu.VMEM((T, Q), BF16),
            pltpu.VMEM((HK, T), F32), pltpu.VMEM((HK, T), F32), pltpu.VMEM((HK, T), F32),
            pltpu.VMEM((T, HK), F32), pltpu.VMEM((T, HK), F32), pltpu.VMEM((T, HK), F32),
            pltpu.VMEM((NK * PLANE, NK), F32),
        ],
        compiler_params=_params(("parallel",)),
    )(n2, wq_bf, keys_bf)


def _expert_kernel(n_ref, u_ref, v_ref, g_ref, o_ref, *, sub):
    k = pl.program_id(1)

    @pl.when(k == 0)
    def _():
        o_ref[...] = jnp.zeros_like(o_ref)

    n = n_ref[...]
    te = u_ref.shape[0]
    acc = None
    for c in range(te // sub):
        cs = slice(c * sub, (c + 1) * sub)
        act = lax.dot_general(n, u_ref[cs, :], _NT, preferred_element_type=F32)
        a = (_gelu(act) * g_ref[:, cs].astype(F32)).astype(BF16)
        part = jnp.dot(a, v_ref[cs, :], preferred_element_type=F32)
        acc = part if acc is None else acc + part
    o_ref[...] += acc


def _experts(n2, u_bf, v_bf, gates, tt, te, sub):
    M, D = n2.shape
    E = u_bf.shape[0]
    return pl.pallas_call(
        functools.partial(_expert_kernel, sub=sub),
        grid=(M // tt, E // te),
        in_specs=[
            pl.BlockSpec((tt, D), lambda i, k: (i, 0)),
            pl.BlockSpec((te, D), lambda i, k: (k, 0)),
            pl.BlockSpec((te, D), lambda i, k: (k, 0)),
            pl.BlockSpec((tt, te), lambda i, k: (i, k)),
        ],
        out_specs=pl.BlockSpec((tt, D), lambda i, k: (i, 0)),
        out_shape=jax.ShapeDtypeStruct((M, D), F32),
        compiler_params=_params(("parallel", "arbitrary")),
    )(n2, u_bf, v_bf, gates)


def _finish_kernel(h_ref, p_ref, g_ref, o_ref, *, final_norm):
    h2 = h_ref[...] + p_ref[...]
    if final_norm:
        ms = jnp.mean(h2 * h2, axis=-1, keepdims=True)
        h2 = h2 * lax.rsqrt(ms + EPS) * g_ref[...]
    o_ref[...] = h2


def _finish(h1, peer_out, g, tm, final_norm):
    M, D = h1.shape
    return pl.pallas_call(
        functools.partial(_finish_kernel, final_norm=final_norm),
        grid=(M // tm,),
        in_specs=[
            pl.BlockSpec((tm, D), lambda i: (i, 0)),
            pl.BlockSpec((tm, D), lambda i: (i, 0)),
            pl.BlockSpec((1, D), lambda i: (0, 0)),
        ],
        out_specs=pl.BlockSpec((tm, D), lambda i: (i, 0)),
        out_shape=jax.ShapeDtypeStruct((M, D), F32),
        compiler_params=_params(("parallel",)),
    )(h1, peer_out, g.reshape(1, D))


def _tile(n, want):
    t = min(n, want)
    assert n % t == 0, (n, t)
    return t


def kernel(x, mix_norm_g, w_in, conv_w, conv_b, lru_w_a, lru_b_a, lru_w_x, lru_b_x, lru_lambda,
           gmlp_ln_g, gmlp_ln_b, gmlp_w_s, gmlp_b_s, w_out, ffn_norm_g, peer_w_q, peer_sub_keys,
           peer_u, peer_v, final_norm_g):
    B, S, D = x.shape
    M = B * S
    depth = w_in.shape[0]
    assert S % CHUNK == 0 and M % ROUTE_T == 0
    h = x.reshape(M, D)
    for l in range(depth):
        z4 = _mix_in(h, mix_norm_g[l], w_in[l].astype(BF16), _tile(M, 512))
        cw = z4.shape[1] // 4
        hf, hb = _lru(z4.reshape(B, S, 4 * cw), conv_w[l], conv_b[l],
                      lru_w_a[l].astype(BF16), lru_b_a[l], lru_w_x[l].astype(BF16), lru_b_x[l],
                      lru_lambda[l], _tile(S, 256))
        bs_full = jnp.repeat(gmlp_b_s[l].T, cw // GMLP_HEADS, axis=1)
        b_out = _gmlp(z4, gmlp_ln_g[l], gmlp_ln_b[l], gmlp_w_s[l].astype(BF16), bs_full, _tile(M, 512))
        h1, n2 = _out_proj(hf.reshape(M, cw), hb.reshape(M, cw), z4, b_out, h,
                           w_out[l].astype(BF16), ffn_norm_g[l], _tile(M, 256))
        keys = peer_sub_keys[l].reshape(PEER_HEADS * 2, PEER_KEYS, -1).astype(BF16)
        gates = _route(n2, peer_w_q[l].astype(BF16), keys)
        peer_out = _experts(n2, peer_u[l].astype(BF16), peer_v[l].astype(BF16), gates,
                            _tile(M, 1024), 512, 256)
        h = _finish(h1, peer_out, final_norm_g, _tile(M, 512), l == depth - 1)
    return h.reshape(B, S, D)
```

```python
import functools

import jax
import jax.numpy as jnp
from jax import lax
from jax.experimental import pallas as pl
from jax.experimental.pallas import tpu as pltpu

F32 = jnp.float32
BF16 = jnp.bfloat16

EPS = 1e-6
LRU_C = 8.0
LRU_HEADS = 8
CONV_WIDTH = 4
CONV_LEFT = 2
GMLP_HEADS = 8
CHUNK = 128
PEER_HEADS = 8
PEER_KEYS = 128
PEER_TOPK = 16

LANES = 128
SUBLANES = 8
VMEM_LIMIT = 56 * 1024 * 1024

_NT = (((1,), (1,)), ((), ()))


def _gelu(x):
    c = 0.7978845608028654
    return 0.5 * x * (1.0 + jnp.tanh(c * (x + 0.044715 * (x * x * x))))


def _sigmoid(x):
    return 1.0 / (1.0 + jnp.exp(-x))


def _params(sem):
    return pltpu.CompilerParams(dimension_semantics=sem, vmem_limit_bytes=VMEM_LIMIT)


def _mix_in_kernel(x_ref, g_ref, w_ref, o_ref, n_scr):
    j = pl.program_id(1)

    @pl.when(j == 0)
    def _():
        x = x_ref[...]
        ms = jnp.mean(x * x, axis=-1, keepdims=True)
        n_scr[...] = (x * lax.rsqrt(ms + EPS) * g_ref[...]).astype(BF16)

    acc = jnp.dot(n_scr[...], w_ref[...], preferred_element_type=F32)

    @pl.when(j == 0)
    def _():
        o_ref[...] = acc

    @pl.when(j > 0)
    def _():
        o_ref[...] = _gelu(acc)


def _mix_in(x2, g, w_bf, tm):
    M, D = x2.shape
    N = w_bf.shape[1]
    tn = N // 4
    return pl.pallas_call(
        _mix_in_kernel,
        grid=(M // tm, 4),
        in_specs=[
            pl.BlockSpec((tm, D), lambda i, j: (i, 0)),
            pl.BlockSpec((1, D), lambda i, j: (0, 0)),
            pl.BlockSpec((D, tn), lambda i, j: (0, j)),
        ],
        out_specs=pl.BlockSpec((tm, tn), lambda i, j: (i, j)),
        out_shape=jax.ShapeDtypeStruct((M, N), F32),
        scratch_shapes=[pltpu.VMEM((tm, D), BF16)],
        compiler_params=_params(("parallel", "arbitrary")),
    )(x2, g.reshape(1, D), w_bf)


def _lru_kernel(xf_ref, xfp_ref, xfn_ref, xb_ref, xbp_ref, xbn_ref,
                cw_ref, cb_ref, wa_ref, ba_ref, wx_ref, bx_ref, lam_ref,
                hf_ref, hb_ref,
                xe_scr, a_scr, u_scr, cf_scr, cr_scr, *, tt, nt):
    j = pl.program_id(1)
    C = xf_ref.shape[-1]
    hd = C // LRU_HEADS

    @pl.when(j == 0)
    def _():
        cf_scr[...] = jnp.zeros_like(cf_scr)
        cr_scr[...] = jnp.zeros_like(cr_scr)

    def gates(cur_ref, prev_ref, next_ref, jt, d):
        xe_scr[0:SUBLANES, :] = jnp.where(jt > 0, prev_ref[0], 0.0)
        xe_scr[SUBLANES:SUBLANES + tt, :] = cur_ref[0]
        xe_scr[SUBLANES + tt:2 * SUBLANES + tt, :] = jnp.where(jt < nt - 1, next_ref[0], 0.0)
        xc = cb_ref[...] + jnp.zeros((tt, C), F32)
        for k in range(CONV_WIDTH):
            off = SUBLANES + k - CONV_LEFT
            xc = xc + xe_scr[off:off + tt, :] * cw_ref[k:k + 1, :]
        xcb = xc.astype(BF16)
        ra = jnp.concatenate(
            [jnp.dot(xcb[:, h * hd:(h + 1) * hd], wa_ref[d, h], preferred_element_type=F32)
             for h in range(LRU_HEADS)], axis=1) + ba_ref[d:d + 1, :]
        rx = jnp.concatenate(
            [jnp.dot(xcb[:, h * hd:(h + 1) * hd], wx_ref[d, h], preferred_element_type=F32)
             for h in range(LRU_HEADS)], axis=1) + bx_ref[d:d + 1, :]
        r = _sigmoid(ra)
        i = _sigmoid(rx)
        nl = -lam_ref[d:d + 1, :]
        sp = jnp.maximum(nl, 0.0) + jnp.log(1.0 + jnp.exp(-jnp.abs(nl)))
        a = jnp.exp(-LRU_C * sp * r)
        a_scr[...] = a
        u_scr[...] = jnp.sqrt(1.0 - a * a) * (i * xc)

    row = lax.broadcasted_iota(jnp.int32, (SUBLANES, C), 0)
    ng = tt // SUBLANES

    gates(xf_ref, xfp_ref, xfn_ref, j, 0)

    def fwd_body(g, carry):
        r0 = pl.multiple_of(g * SUBLANES, SUBLANES)
        a = a_scr[pl.ds(r0, SUBLANES), :]
        u = u_scr[pl.ds(r0, SUBLANES), :]
        for s in (1, 2, 4):
            m = row >= s
            u = u + a * jnp.where(m, pltpu.roll(u, s, 0), 0.0)
            a = a * jnp.where(m, pltpu.roll(a, s, 0), 1.0)
        h = u + a * carry
        hf_ref[0, pl.ds(r0, SUBLANES), :] = h
        return h[SUBLANES - 1:SUBLANES, :]

    cf_scr[...] = lax.fori_loop(0, ng, fwd_body, cf_scr[...])

    gates(xb_ref, xbp_ref, xbn_ref, nt - 1 - j, 1)

    def bwd_body(gi, carry):
        g = ng - 1 - gi
        r0 = pl.multiple_of(g * SUBLANES, SUBLANES)
        a = a_scr[pl.ds(r0, SUBLANES), :]
        u = u_scr[pl.ds(r0, SUBLANES), :]
        for s in (1, 2, 4):
            m = row < SUBLANES - s
            u = u + a * jnp.where(m, pltpu.roll(u, SUBLANES - s, 0), 0.0)
            a = a * jnp.where(m, pltpu.roll(a, SUBLANES - s, 0), 1.0)
        h = u + a * carry
        hb_ref[0, pl.ds(r0, SUBLANES), :] = h
        return h[0:1, :]

    cr_scr[...] = lax.fori_loop(0, ng, bwd_body, cr_scr[...])


def _lru(z3, conv_w, conv_b, w_a_bf, b_a, w_x_bf, b_x, lam, tt):
    B, S, _ = z3.shape
    C = conv_w.shape[-1]
    nt = S // tt
    tb = tt // SUBLANES
    nb8 = S // SUBLANES

    def cur(jt):
        return lambda b, j: (b, jt(j), 0)

    def prev(jt):
        return lambda b, j: (b, jnp.maximum(jt(j) * tb - 1, 0), 0)

    def nxt(jt):
        return lambda b, j: (b, jnp.minimum((jt(j) + 1) * tb, nb8 - 1), 0)

    fw = lambda j: j
    bw = lambda j: nt - 1 - j
    full = lambda shape: pl.BlockSpec(shape, lambda b, j: (0,) * len(shape))
    return pl.pallas_call(
        functools.partial(_lru_kernel, tt=tt, nt=nt),
        grid=(B, nt),
        in_specs=[
            pl.BlockSpec((1, tt, C), cur(fw)),
            pl.BlockSpec((1, SUBLANES, C), prev(fw)),
            pl.BlockSpec((1, SUBLANES, C), nxt(fw)),
            pl.BlockSpec((1, tt, C), cur(bw)),
            pl.BlockSpec((1, SUBLANES, C), prev(bw)),
            pl.BlockSpec((1, SUBLANES, C), nxt(bw)),
            full((CONV_WIDTH, C)), full((1, C)),
            full(w_a_bf.shape), full((2, C)), full(w_x_bf.shape), full((2, C)), full((2, C)),
        ],
        out_specs=[
            pl.BlockSpec((1, tt, C), cur(fw)),
            pl.BlockSpec((1, tt, C), cur(bw)),
        ],
        out_shape=[jax.ShapeDtypeStruct((B, S, C), F32), jax.ShapeDtypeStruct((B, S, C), F32)],
        scratch_shapes=[
            pltpu.VMEM((tt + 2 * SUBLANES, C), F32),
            pltpu.VMEM((tt, C), F32),
            pltpu.VMEM((tt, C), F32),
            pltpu.VMEM((1, C), F32),
            pltpu.VMEM((1, C), F32),
        ],
        compiler_params=_params(("parallel", "arbitrary")),
    )(z3, z3, z3, z3, z3, z3, conv_w, conv_b.reshape(1, C), w_a_bf, b_a, w_x_bf, b_x, lam)


def _gmlp_kernel(u_ref, v_ref, lng_ref, lnb_ref, ws_ref, bs_ref, o_ref, *, tc):
    v = v_ref[...]
    mu = jnp.mean(v, axis=-1, keepdims=True)
    vc = v - mu
    var = jnp.mean(vc * vc, axis=-1, keepdims=True)
    vb = (vc * lax.rsqrt(var + EPS) * lng_ref[...] + lnb_ref[...]).astype(BF16)
    hd = v.shape[-1] // GMLP_HEADS
    for c in range(tc // CHUNK):
        rows = slice(c * CHUNK, (c + 1) * CHUNK)
        mixed = jnp.concatenate(
            [jnp.dot(ws_ref[h], vb[rows, h * hd:(h + 1) * hd], preferred_element_type=F32)
             for h in range(GMLP_HEADS)], axis=1)
        o_ref[rows, :] = (u_ref[rows, :] * (mixed + bs_ref[...])).astype(BF16)


def _gmlp(z4, ln_g, ln_b, w_s_bf, bs_full, tc):
    M = z4.shape[0]
    C = ln_g.shape[-1]
    return pl.pallas_call(
        functools.partial(_gmlp_kernel, tc=tc),
        grid=(M // tc,),
        in_specs=[
            pl.BlockSpec((tc, C), lambda i: (i, 2)),
            pl.BlockSpec((tc, C), lambda i: (i, 3)),
            pl.BlockSpec((1, C), lambda i: (0, 0)),
            pl.BlockSpec((1, C), lambda i: (0, 0)),
            pl.BlockSpec(w_s_bf.shape, lambda i: (0, 0, 0)),
            pl.BlockSpec((CHUNK, C), lambda i: (0, 0)),
        ],
        out_specs=pl.BlockSpec((tc, C), lambda i: (i, 0)),
        out_shape=jax.ShapeDtypeStruct((M, C), BF16),
        compiler_params=_params(("parallel",)),
    )(z4, z4, ln_g.reshape(1, C), ln_b.reshape(1, C), w_s_bf, bs_full)


def _out_proj_kernel(hf_ref, hb_ref, gr_ref, bo_ref, x_ref, w_ref, g_ref, h1_ref, n2_ref):
    ca = hf_ref.shape[-1]
    a_out = ((hf_ref[...] + hb_ref[...]) * gr_ref[...]).astype(BF16)
    acc = jnp.dot(a_out, w_ref[0:ca, :], preferred_element_type=F32)
    acc = acc + jnp.dot(bo_ref[...], w_ref[ca:, :], preferred_element_type=F32)
    h1 = x_ref[...] + acc
    h1_ref[...] = h1
    ms = jnp.mean(h1 * h1, axis=-1, keepdims=True)
    n2_ref[...] = (h1 * lax.rsqrt(ms + EPS) * g_ref[...]).astype(BF16)


def _out_proj(hf, hb, z4, b_out, x2, w_bf, g, tm):
    M, D = x2.shape
    ca = hf.shape[-1]
    cb = b_out.shape[-1]
    return pl.pallas_call(
        _out_proj_kernel,
        grid=(M // tm,),
        in_specs=[
            pl.BlockSpec((tm, ca), lambda i: (i, 0)),
            pl.BlockSpec((tm, ca), lambda i: (i, 0)),
            pl.BlockSpec((tm, ca), lambda i: (i, 1)),
            pl.BlockSpec((tm, cb), lambda i: (i, 0)),
            pl.BlockSpec((tm, D), lambda i: (i, 0)),
            pl.BlockSpec(w_bf.shape, lambda i: (0, 0)),
            pl.BlockSpec((1, D), lambda i: (0, 0)),
        ],
        out_specs=[pl.BlockSpec((tm, D), lambda i: (i, 0)), pl.BlockSpec((tm, D), lambda i: (i, 0))],
        out_shape=[jax.ShapeDtypeStruct((M, D), F32), jax.ShapeDtypeStruct((M, D), BF16)],
        compiler_params=_params(("parallel",)),
    )(hf, hb, z4, b_out, x2, w_bf, g.reshape(1, D))


ROUTE_T = LANES
PLANE = ROUTE_T + SUBLANES


def _batcher_pairs(n):
    pairs = []

    def merge(lo, m, r):
        step = r * 2
        if step < m:
            merge(lo, m, step)
            merge(lo + r, m, step)
            pairs.extend((i, i + r) for i in range(lo + r, lo + m - r, step))
        else:
            pairs.append((lo, lo + r))

    def sort(lo, m):
        if m > 1:
            sort(lo, m // 2)
            sort(lo + m // 2, m // 2)
            merge(lo, m, 1)

    sort(0, n)
    return pairs


def _beats(x, y):
    return (x[0] > y[0]) | ((x[0] == y[0]) & (x[1] < y[1]))


def _best(x, y):
    g = _beats(x, y)
    return (jnp.maximum(x[0], y[0]),) + tuple(jnp.where(g, p, q) for p, q in zip(x[1:], y[1:]))


def _compare_exchange(x, y):
    g = _beats(x, y)
    hi = (jnp.maximum(x[0], y[0]),) + tuple(jnp.where(g, p, q) for p, q in zip(x[1:], y[1:]))
    lo = (jnp.minimum(x[0], y[0]),) + tuple(jnp.where(g, q, p) for p, q in zip(x[1:], y[1:]))
    return hi, lo


def _sort_desc(items):
    items = list(items)
    for i, j in _batcher_pairs(len(items)):
        items[i], items[j] = _compare_exchange(items[i], items[j])
    return items


def _bitonic_merge(items):
    items = list(items)
    n = len(items)
    d = n // 2
    while d >= 1:
        for i in range(n):
            if i & d == 0:
                items[i], items[i + d] = _compare_exchange(items[i], items[i + d])
        d //= 2
    return items


def _merge_top(x, y):
    n = len(x)
    return _bitonic_merge([_best(x[k], y[n - 1 - k]) for k in range(n)])


def _merge_full(x, y):
    return _bitonic_merge(list(x) + list(reversed(y)))


def _route_kernel(n_ref, wq_ref, keys_ref, g_ref, q_scr, sv_scr, si_scr, w_scr, i_scr, j_scr,
                  wt_scr, it_scr, jt_scr, gs_scr):
    T = ROUTE_T
    K = PEER_TOPK
    NK = PEER_KEYS
    q_scr[...] = jnp.dot(n_ref[...], wq_ref[...], preferred_element_type=F32).astype(BF16)

    H = PEER_HEADS
    row8 = lax.broadcasted_iota(jnp.int32, (SUBLANES, T), 0).astype(F32)

    def unit_body(u, _):
        c0 = pl.multiple_of(u * NK, NK)
        s = lax.dot_general(keys_ref[u], q_scr[:, pl.ds(c0, NK)], _NT, preferred_element_type=F32)
        items = [(s[b * SUBLANES:(b + 1) * SUBLANES, :], row8 + float(b * SUBLANES))
                 for b in range(NK // SUBLANES)]
        items = _sort_desc(items)
        half = jnp.bitwise_and(u, 1)
        head = jnp.right_shift(u, 1)
        for k, (v, i) in enumerate(items):
            r0 = pl.multiple_of(((half * K + k) * H + head) * SUBLANES, SUBLANES)
            sv_scr[pl.ds(r0, SUBLANES), :] = v
            si_scr[pl.ds(r0, SUBLANES), :] = i
        return 0

    lax.fori_loop(0, 2 * H, unit_body, 0)

    def half_top(half):
        lists = []
        for r in range(SUBLANES):
            lst = []
            for k in range(K):
                start = (half * K + k) * H * SUBLANES + r
                lst.append((sv_scr[pl.ds(start, H, stride=SUBLANES), :],
                            si_scr[pl.ds(start, H, stride=SUBLANES), :]))
            lists.append(lst)
        while len(lists) > 1:
            lists = [_merge_top(lists[m], lists[m + 1]) for m in range(0, len(lists), 2)]
        return lists[0]

    a = half_top(0)
    b = half_top(1)

    def pair(p, r):
        code = jnp.full((H, T), float(p * K + r), F32)
        return (a[p][0] + b[r][0], code, a[p][1] * float(NK) + b[r][1])

    rows = [[pair(p, r) for r in range(K // (p + 1))] for p in range(K)]
    singles = [rows[p][0] for p in range(8, K)]
    l1 = _merge_full(rows[1], singles)
    l2 = _sort_desc(rows[2] + rows[3] + rows[4] + rows[5] + rows[6])
    top = _merge_top(rows[0], l1)
    top = _merge_top(top, l2)
    top[K - 2] = _best(top[K - 2], rows[7][1])
    top[K - 1] = _best(top[K - 1], rows[7][0])
    top = _bitonic_merge(top)

    es = [jnp.exp(t[0] - top[0][0]) for t in top]
    z = es[0]
    for e in es[1:]:
        z = z + e
    inv = 1.0 / z
    for k in range(K):
        ident = top[k][2]
        ii = jnp.floor(ident * (1.0 / NK))
        w_scr[k * H:(k + 1) * H, :] = es[k] * inv
        i_scr[k * H:(k + 1) * H, :] = ii
        j_scr[k * H:(k + 1) * H, :] = ident - ii * float(NK)

    wt_scr[...] = w_scr[...].T
    it_scr[...] = i_scr[...].T
    jt_scr[...] = j_scr[...].T

    sub = lax.broadcasted_iota(jnp.int32, (NK, PEER_HEADS * K), 0).astype(F32)

    def tok_body(t, _):
        wrow = wt_scr[pl.ds(t, 1), :]
        irow = it_scr[pl.ds(t, 1), :]
        jrow = jt_scr[pl.ds(t, 1), :]
        p1 = jnp.where(sub == irow, wrow, 0.0).astype(BF16)
        p2 = jnp.where(sub == jrow, 1.0, 0.0).astype(BF16)
        g = lax.dot_general(p1, p2, _NT, preferred_element_type=F32)
        for blk in range(NK // SUBLANES):
            gs_scr[pl.ds(blk * SUBLANES * PLANE + t, SUBLANES, stride=PLANE), :] = (
                g[blk * SUBLANES:(blk + 1) * SUBLANES, :])
        return 0

    lax.fori_loop(0, T, tok_body, 0, unroll=8)

    for i in range(NK):
        g_ref[:, i * NK:(i + 1) * NK] = gs_scr[i * PLANE:i * PLANE + T, :].astype(BF16)


def _route(n2, wq_bf, keys_bf):
    M, D = n2.shape
    Q = wq_bf.shape[1]
    T = ROUTE_T
    NK = PEER_KEYS
    HK = PEER_HEADS * PEER_TOPK
    return pl.pallas_call(
        _route_kernel,
        grid=(M // T,),
        in_specs=[
            pl.BlockSpec((T, D), lambda i: (i, 0)),
            pl.BlockSpec((D, Q), lambda i: (0, 0)),
            pl.BlockSpec(keys_bf.shape, lambda i: (0, 0, 0)),
        ],
        out_specs=pl.BlockSpec((T, NK * NK), lambda i: (i, 0)),
        out_shape=jax.ShapeDtypeStruct((M, NK * NK), BF16),
        scratch_shapes=[
            pltpu.VMEM((T, Q), BF16),
            pltpu.VMEM((2 * HK * SUBLANES, T), F32), pltpu.VMEM((2 * HK * SUBLANES, T), F32),
            pltpu.VMEM((HK, T), F32), pltpu.VMEM((HK, T), F32), pltpu.VMEM((HK, T), F32),
            pltpu.VMEM((T, HK), F32), pltpu.VMEM((T, HK), F32), pltpu.VMEM((T, HK), F32),
            pltpu.VMEM((NK * PLANE, NK), F32),
        ],
        compiler_params=_params(("parallel",)),
    )(n2, wq_bf, keys_bf)


def _expert_kernel(n_ref, u_ref, v_ref, g_ref, o_ref, *, sub):
    k = pl.program_id(1)

    @pl.when(k == 0)
    def _():
        o_ref[...] = jnp.zeros_like(o_ref)

    n = n_ref[...]
    te = u_ref.shape[0]
    acc = None
    for c in range(te // sub):
        cs = slice(c * sub, (c + 1) * sub)
        act = lax.dot_general(n, u_ref[cs, :], _NT, preferred_element_type=F32)
        a = (_gelu(act) * g_ref[:, cs].astype(F32)).astype(BF16)
        part = jnp.dot(a, v_ref[cs, :], preferred_element_type=F32)
        acc = part if acc is None else acc + part
    o_ref[...] += acc


def _experts(n2, u_bf, v_bf, gates, tt, te, sub):
    M, D = n2.shape
    E = u_bf.shape[0]
    return pl.pallas_call(
        functools.partial(_expert_kernel, sub=sub),
        grid=(M // tt, E // te),
        in_specs=[
            pl.BlockSpec((tt, D), lambda i, k: (i, 0)),
            pl.BlockSpec((te, D), lambda i, k: (k, 0)),
            pl.BlockSpec((te, D), lambda i, k: (k, 0)),
            pl.BlockSpec((tt, te), lambda i, k: (i, k)),
        ],
        out_specs=pl.BlockSpec((tt, D), lambda i, k: (i, 0)),
        out_shape=jax.ShapeDtypeStruct((M, D), F32),
        compiler_params=_params(("parallel", "arbitrary")),
    )(n2, u_bf, v_bf, gates)


def _finish_kernel(h_ref, p_ref, g_ref, o_ref, *, final_norm):
    h2 = h_ref[...] + p_ref[...]
    if final_norm:
        ms = jnp.mean(h2 * h2, axis=-1, keepdims=True)
        h2 = h2 * lax.rsqrt(ms + EPS) * g_ref[...]
    o_ref[...] = h2


def _finish(h1, peer_out, g, tm, final_norm):
    M, D = h1.shape
    return pl.pallas_call(
        functools.partial(_finish_kernel, final_norm=final_norm),
        grid=(M // tm,),
        in_specs=[
            pl.BlockSpec((tm, D), lambda i: (i, 0)),
            pl.BlockSpec((tm, D), lambda i: (i, 0)),
            pl.BlockSpec((1, D), lambda i: (0, 0)),
        ],
        out_specs=pl.BlockSpec((tm, D), lambda i: (i, 0)),
        out_shape=jax.ShapeDtypeStruct((M, D), F32),
        compiler_params=_params(("parallel",)),
    )(h1, peer_out, g.reshape(1, D))


def _tile(n, want):
    t = min(n, want)
    assert n % t == 0, (n, t)
    return t


def kernel(x, mix_norm_g, w_in, conv_w, conv_b, lru_w_a, lru_b_a, lru_w_x, lru_b_x, lru_lambda,
           gmlp_ln_g, gmlp_ln_b, gmlp_w_s, gmlp_b_s, w_out, ffn_norm_g, peer_w_q, peer_sub_keys,
           peer_u, peer_v, final_norm_g):
    B, S, D = x.shape
    M = B * S
    depth = w_in.shape[0]
    assert S % CHUNK == 0 and M % ROUTE_T == 0
    h = x.reshape(M, D)
    for l in range(depth):
        z4 = _mix_in(h, mix_norm_g[l], w_in[l].astype(BF16), _tile(M, 512))
        cw = z4.shape[1] // 4
        hf, hb = _lru(z4.reshape(B, S, 4 * cw), conv_w[l], conv_b[l],
                      lru_w_a[l].astype(BF16), lru_b_a[l], lru_w_x[l].astype(BF16), lru_b_x[l],
                      lru_lambda[l], _tile(S, 256))
        bs_full = jnp.repeat(gmlp_b_s[l].T, cw // GMLP_HEADS, axis=1)
        b_out = _gmlp(z4, gmlp_ln_g[l], gmlp_ln_b[l], gmlp_w_s[l].astype(BF16), bs_full, _tile(M, 512))
        h1, n2 = _out_proj(hf.reshape(M, cw), hb.reshape(M, cw), z4, b_out, h,
                           w_out[l].astype(BF16), ffn_norm_g[l], _tile(M, 256))
        keys = peer_sub_keys[l].reshape(PEER_HEADS * 2, PEER_KEYS, -1).astype(BF16)
        gates = _route(n2, peer_w_q[l].astype(BF16), keys)
        peer_out = _experts(n2, peer_u[l].astype(BF16), peer_v[l].astype(BF16), gates,
                            _tile(M, 1024), 512, 256)
        h = _finish(h1, peer_out, final_norm_g, _tile(M, 512), l == depth - 1)
    return h.reshape(B, S, D)
```

```python
import functools

import jax
import jax.numpy as jnp
from jax import lax
from jax.experimental import pallas as pl
from jax.experimental.pallas import tpu as pltpu

F32 = jnp.float32
BF16 = jnp.bfloat16

EPS = 1e-6
LRU_C = 8.0
LRU_HEADS = 8
CONV_WIDTH = 4
CONV_LEFT = 2
GMLP_HEADS = 8
CHUNK = 128
PEER_HEADS = 8
PEER_KEYS = 128
PEER_TOPK = 16

LANES = 128
SUBLANES = 8
VMEM_LIMIT = 56 * 1024 * 1024

_NT = (((1,), (1,)), ((), ()))


def _gelu(x):
    c = 0.7978845608028654
    return 0.5 * x * (1.0 + jnp.tanh(c * (x + 0.044715 * (x * x * x))))


def _twice_gelu(x):
    c = 0.7978845608028654
    return x * (1.0 + jnp.tanh(x * (c + (c * 0.044715) * (x * x))))


def _sigmoid(x):
    return 1.0 / (1.0 + jnp.exp(-x))


def _params(sem):
    return pltpu.CompilerParams(dimension_semantics=sem, vmem_limit_bytes=VMEM_LIMIT)


def _mix_in_kernel(x_ref, g_ref, w_ref, o_ref, n_scr):
    j = pl.program_id(1)

    @pl.when(j == 0)
    def _():
        x = x_ref[...]
        ms = jnp.mean(x * x, axis=-1, keepdims=True)
        n_scr[...] = (x * lax.rsqrt(ms + EPS) * g_ref[...]).astype(BF16)

    acc = jnp.dot(n_scr[...], w_ref[...], preferred_element_type=F32)

    @pl.when(j == 0)
    def _():
        o_ref[...] = acc

    @pl.when(j > 0)
    def _():
        o_ref[...] = _gelu(acc)


def _mix_in(x2, g, w_bf, tm):
    M, D = x2.shape
    N = w_bf.shape[1]
    tn = N // 4
    return pl.pallas_call(
        _mix_in_kernel,
        grid=(M // tm, 4),
        in_specs=[
            pl.BlockSpec((tm, D), lambda i, j: (i, 0)),
            pl.BlockSpec((1, D), lambda i, j: (0, 0)),
            pl.BlockSpec((D, tn), lambda i, j: (0, j)),
        ],
        out_specs=pl.BlockSpec((tm, tn), lambda i, j: (i, j)),
        out_shape=jax.ShapeDtypeStruct((M, N), F32),
        scratch_shapes=[pltpu.VMEM((tm, D), BF16)],
        compiler_params=_params(("parallel", "arbitrary")),
    )(x2, g.reshape(1, D), w_bf)


def _lru_kernel(xf_ref, xfp_ref, xfn_ref, xb_ref, xbp_ref, xbn_ref,
                cw_ref, cb_ref, wa_ref, ba_ref, wx_ref, bx_ref, lam_ref,
                hf_ref, hb_ref,
                xe_scr, a_scr, u_scr, cf_scr, cr_scr, *, tt, nt):
    j = pl.program_id(1)
    C = xf_ref.shape[-1]
    hd = C // LRU_HEADS

    @pl.when(j == 0)
    def _():
        cf_scr[...] = jnp.zeros_like(cf_scr)
        cr_scr[...] = jnp.zeros_like(cr_scr)

    def gates(cur_ref, prev_ref, next_ref, jt, d):
        xe_scr[0:SUBLANES, :] = jnp.where(jt > 0, prev_ref[0], 0.0)
        xe_scr[SUBLANES:SUBLANES + tt, :] = cur_ref[0]
        xe_scr[SUBLANES + tt:2 * SUBLANES + tt, :] = jnp.where(jt < nt - 1, next_ref[0], 0.0)
        xc = cb_ref[...] + jnp.zeros((tt, C), F32)
        for k in range(CONV_WIDTH):
            off = SUBLANES + k - CONV_LEFT
            xc = xc + xe_scr[off:off + tt, :] * cw_ref[k:k + 1, :]
        xcb = xc.astype(BF16)
        ra = jnp.concatenate(
            [jnp.dot(xcb[:, h * hd:(h + 1) * hd], wa_ref[d, h], preferred_element_type=F32)
             for h in range(LRU_HEADS)], axis=1) + ba_ref[d:d + 1, :]
        rx = jnp.concatenate(
            [jnp.dot(xcb[:, h * hd:(h + 1) * hd], wx_ref[d, h], preferred_element_type=F32)
             for h in range(LRU_HEADS)], axis=1) + bx_ref[d:d + 1, :]
        r = _sigmoid(ra)
        i = _sigmoid(rx)
        nl = -lam_ref[d:d + 1, :]
        sp = jnp.maximum(nl, 0.0) + jnp.log(1.0 + jnp.exp(-jnp.abs(nl)))
        a = jnp.exp(-LRU_C * sp * r)
        a_scr[...] = a
        u_scr[...] = jnp.sqrt(1.0 - a * a) * (i * xc)

    row = lax.broadcasted_iota(jnp.int32, (SUBLANES, C), 0)
    ng = tt // SUBLANES

    gates(xf_ref, xfp_ref, xfn_ref, j, 0)

    def fwd_body(g, carry):
        r0 = pl.multiple_of(g * SUBLANES, SUBLANES)
        a = a_scr[pl.ds(r0, SUBLANES), :]
        u = u_scr[pl.ds(r0, SUBLANES), :]
        for s in (1, 2, 4):
            m = row >= s
            u = u + a * jnp.where(m, pltpu.roll(u, s, 0), 0.0)
            a = a * jnp.where(m, pltpu.roll(a, s, 0), 1.0)
        h = u + a * carry
        hf_ref[0, pl.ds(r0, SUBLANES), :] = h
        return h[SUBLANES - 1:SUBLANES, :]

    cf_scr[...] = lax.fori_loop(0, ng, fwd_body, cf_scr[...])

    gates(xb_ref, xbp_ref, xbn_ref, nt - 1 - j, 1)

    def bwd_body(gi, carry):
        g = ng - 1 - gi
        r0 = pl.multiple_of(g * SUBLANES, SUBLANES)
        a = a_scr[pl.ds(r0, SUBLANES), :]
        u = u_scr[pl.ds(r0, SUBLANES), :]
        for s in (1, 2, 4):
            m = row < SUBLANES - s
            u = u + a * jnp.where(m, pltpu.roll(u, SUBLANES - s, 0), 0.0)
            a = a * jnp.where(m, pltpu.roll(a, SUBLANES - s, 0), 1.0)
        h = u + a * carry
        hb_ref[0, pl.ds(r0, SUBLANES), :] = h
        return h[0:1, :]

    cr_scr[...] = lax.fori_loop(0, ng, bwd_body, cr_scr[...])


def _lru(z3, conv_w, conv_b, w_a_bf, b_a, w_x_bf, b_x, lam, tt):
    B, S, _ = z3.shape
    C = conv_w.shape[-1]
    nt = S // tt
    tb = tt // SUBLANES
    nb8 = S // SUBLANES

    def cur(jt):
        return lambda b, j: (b, jt(j), 0)

    def prev(jt):
        return lambda b, j: (b, jnp.maximum(jt(j) * tb - 1, 0), 0)

    def nxt(jt):
        return lambda b, j: (b, jnp.minimum((jt(j) + 1) * tb, nb8 - 1), 0)

    fw = lambda j: j
    bw = lambda j: nt - 1 - j
    full = lambda shape: pl.BlockSpec(shape, lambda b, j: (0,) * len(shape))
    return pl.pallas_call(
        functools.partial(_lru_kernel, tt=tt, nt=nt),
        grid=(B, nt),
        in_specs=[
            pl.BlockSpec((1, tt, C), cur(fw)),
            pl.BlockSpec((1, SUBLANES, C), prev(fw)),
            pl.BlockSpec((1, SUBLANES, C), nxt(fw)),
            pl.BlockSpec((1, tt, C), cur(bw)),
            pl.BlockSpec((1, SUBLANES, C), prev(bw)),
            pl.BlockSpec((1, SUBLANES, C), nxt(bw)),
            full((CONV_WIDTH, C)), full((1, C)),
            full(w_a_bf.shape), full((2, C)), full(w_x_bf.shape), full((2, C)), full((2, C)),
        ],
        out_specs=[
            pl.BlockSpec((1, tt, C), cur(fw)),
            pl.BlockSpec((1, tt, C), cur(bw)),
        ],
        out_shape=[jax.ShapeDtypeStruct((B, S, C), F32), jax.ShapeDtypeStruct((B, S, C), F32)],
        scratch_shapes=[
            pltpu.VMEM((tt + 2 * SUBLANES, C), F32),
            pltpu.VMEM((tt, C), F32),
            pltpu.VMEM((tt, C), F32),
            pltpu.VMEM((1, C), F32),
            pltpu.VMEM((1, C), F32),
        ],
        compiler_params=_params(("parallel", "arbitrary")),
    )(z3, z3, z3, z3, z3, z3, conv_w, conv_b.reshape(1, C), w_a_bf, b_a, w_x_bf, b_x, lam)


def _gmlp_kernel(u_ref, v_ref, lng_ref, lnb_ref, ws_ref, bs_ref, o_ref, *, tc):
    v = v_ref[...]
    mu = jnp.mean(v, axis=-1, keepdims=True)
    vc = v - mu
    var = jnp.mean(vc * vc, axis=-1, keepdims=True)
    vb = (vc * lax.rsqrt(var + EPS) * lng_ref[...] + lnb_ref[...]).astype(BF16)
    hd = v.shape[-1] // GMLP_HEADS
    for c in range(tc // CHUNK):
        rows = slice(c * CHUNK, (c + 1) * CHUNK)
        mixed = jnp.concatenate(
            [jnp.dot(ws_ref[h], vb[rows, h * hd:(h + 1) * hd], preferred_element_type=F32)
             for h in range(GMLP_HEADS)], axis=1)
        o_ref[rows, :] = (u_ref[rows, :] * (mixed + bs_ref[...])).astype(BF16)


def _gmlp(z4, ln_g, ln_b, w_s_bf, bs_full, tc):
    M = z4.shape[0]
    C = ln_g.shape[-1]
    return pl.pallas_call(
        functools.partial(_gmlp_kernel, tc=tc),
        grid=(M // tc,),
        in_specs=[
            pl.BlockSpec((tc, C), lambda i: (i, 2)),
            pl.BlockSpec((tc, C), lambda i: (i, 3)),
            pl.BlockSpec((1, C), lambda i: (0, 0)),
            pl.BlockSpec((1, C), lambda i: (0, 0)),
            pl.BlockSpec(w_s_bf.shape, lambda i: (0, 0, 0)),
            pl.BlockSpec((CHUNK, C), lambda i: (0, 0)),
        ],
        out_specs=pl.BlockSpec((tc, C), lambda i: (i, 0)),
        out_shape=jax.ShapeDtypeStruct((M, C), BF16),
        compiler_params=_params(("parallel",)),
    )(z4, z4, ln_g.reshape(1, C), ln_b.reshape(1, C), w_s_bf, bs_full)


def _out_proj_kernel(hf_ref, hb_ref, gr_ref, bo_ref, x_ref, w_ref, g_ref, h1_ref, n2_ref):
    ca = hf_ref.shape[-1]
    a_out = ((hf_ref[...] + hb_ref[...]) * gr_ref[...]).astype(BF16)
    acc = jnp.dot(a_out, w_ref[0:ca, :], preferred_element_type=F32)
    acc = acc + jnp.dot(bo_ref[...], w_ref[ca:, :], preferred_element_type=F32)
    h1 = x_ref[...] + acc
    h1_ref[...] = h1
    ms = jnp.mean(h1 * h1, axis=-1, keepdims=True)
    n2_ref[...] = (h1 * lax.rsqrt(ms + EPS) * g_ref[...]).astype(BF16)


def _out_proj(hf, hb, z4, b_out, x2, w_bf, g, tm):
    M, D = x2.shape
    ca = hf.shape[-1]
    cb = b_out.shape[-1]
    return pl.pallas_call(
        _out_proj_kernel,
        grid=(M // tm,),
        in_specs=[
            pl.BlockSpec((tm, ca), lambda i: (i, 0)),
            pl.BlockSpec((tm, ca), lambda i: (i, 0)),
            pl.BlockSpec((tm, ca), lambda i: (i, 1)),
            pl.BlockSpec((tm, cb), lambda i: (i, 0)),
            pl.BlockSpec((tm, D), lambda i: (i, 0)),
            pl.BlockSpec(w_bf.shape, lambda i: (0, 0)),
            pl.BlockSpec((1, D), lambda i: (0, 0)),
        ],
        out_specs=[pl.BlockSpec((tm, D), lambda i: (i, 0)), pl.BlockSpec((tm, D), lambda i: (i, 0))],
        out_shape=[jax.ShapeDtypeStruct((M, D), F32), jax.ShapeDtypeStruct((M, D), BF16)],
        compiler_params=_params(("parallel",)),
    )(hf, hb, z4, b_out, x2, w_bf, g.reshape(1, D))


ROUTE_T = LANES
PLANE = ROUTE_T + SUBLANES


def _batcher_pairs(n):
    pairs = []

    def merge(lo, m, r):
        step = r * 2
        if step < m:
            merge(lo, m, step)
            merge(lo + r, m, step)
            pairs.extend((i, i + r) for i in range(lo + r, lo + m - r, step))
        else:
            pairs.append((lo, lo + r))

    def sort(lo, m):
        if m > 1:
            sort(lo, m // 2)
            sort(lo + m // 2, m // 2)
            merge(lo, m, 1)

    sort(0, n)
    return pairs


def _beats(x, y):
    return (x[0] > y[0]) | ((x[0] == y[0]) & (x[1] < y[1]))


def _best(x, y):
    g = _beats(x, y)
    return (jnp.maximum(x[0], y[0]),) + tuple(jnp.where(g, p, q) for p, q in zip(x[1:], y[1:]))


def _compare_exchange(x, y):
    g = _beats(x, y)
    hi = (jnp.maximum(x[0], y[0]),) + tuple(jnp.where(g, p, q) for p, q in zip(x[1:], y[1:]))
    lo = (jnp.minimum(x[0], y[0]),) + tuple(jnp.where(g, q, p) for p, q in zip(x[1:], y[1:]))
    return hi, lo


def _sort_desc(items):
    items = list(items)
    for i, j in _batcher_pairs(len(items)):
        items[i], items[j] = _compare_exchange(items[i], items[j])
    return items


def _bitonic_merge(items):
    items = list(items)
    n = len(items)
    d = n // 2
    while d >= 1:
        for i in range(n):
            if i & d == 0:
                items[i], items[i + d] = _compare_exchange(items[i], items[i + d])
        d //= 2
    return items


def _merge_top(x, y):
    n = len(x)
    return _bitonic_merge([_best(x[k], y[n - 1 - k]) for k in range(n)])


def _merge_full(x, y):
    return _bitonic_merge(list(x) + list(reversed(y)))


def _route_kernel(n_ref, wq_ref, keys_ref, g_ref, q_scr, s_scr, sv_scr, si_scr, w_scr, i_scr, j_scr,
                  wt_scr, it_scr, jt_scr, gs_scr):
    T = ROUTE_T
    K = PEER_TOPK
    NK = PEER_KEYS
    q_scr[...] = jnp.dot(n_ref[...], wq_ref[...], preferred_element_type=F32).astype(BF16)

    H = PEER_HEADS
    row8 = lax.broadcasted_iota(jnp.int32, (SUBLANES, T), 0).astype(F32)

    for u in range(2 * H):
        s_scr[u * NK:(u + 1) * NK, :] = lax.dot_general(
            keys_ref[u], q_scr[:, u * NK:(u + 1) * NK], _NT, preferred_element_type=F32)

    def unit_body(u, _):
        c0 = pl.multiple_of(u * NK, NK)
        items = [(s_scr[pl.ds(c0 + b * SUBLANES, SUBLANES), :], row8 + float(b * SUBLANES))
                 for b in range(NK // SUBLANES)]
        items = _sort_desc(items)
        half = jnp.bitwise_and(u, 1)
        head = jnp.right_shift(u, 1)
        for k, (v, i) in enumerate(items):
            r0 = pl.multiple_of(((half * K + k) * H + head) * SUBLANES, SUBLANES)
            sv_scr[pl.ds(r0, SUBLANES), :] = v
            si_scr[pl.ds(r0, SUBLANES), :] = i
        return 0

    lax.fori_loop(0, 2 * H, unit_body, 0)

    def half_top(half):
        def residue_list(r):
            starts = [(half * K + k) * H * SUBLANES + r for k in range(K)]
            return (tuple(sv_scr[pl.ds(st, H, stride=SUBLANES), :] for st in starts),
                    tuple(si_scr[pl.ds(st, H, stride=SUBLANES), :] for st in starts))

        def merge_body(r, acc):
            new = residue_list(r)
            out = _merge_top(list(zip(*acc)), list(zip(*new)))
            return tuple(x[0] for x in out), tuple(x[1] for x in out)

        vals, idxs = lax.fori_loop(1, SUBLANES, merge_body, residue_list(0))
        return list(zip(vals, idxs))

    a = half_top(0)
    b = half_top(1)

    def pair(p, r):
        code = jnp.full((H, T), float(p * K + r), F32)
        return (a[p][0] + b[r][0], code, a[p][1] * float(NK) + b[r][1])

    rows = [[pair(p, r) for r in range(K // (p + 1))] for p in range(K)]
    singles = [rows[p][0] for p in range(8, K)]
    l1 = _merge_full(rows[1], singles)
    l2 = _sort_desc(rows[2] + rows[3] + rows[4] + rows[5] + rows[6])
    top = _merge_top(rows[0], l1)
    top = _merge_top(top, l2)
    top[K - 2] = _best(top[K - 2], rows[7][1])
    top[K - 1] = _best(top[K - 1], rows[7][0])
    top = _bitonic_merge(top)

    es = [jnp.exp(t[0] - top[0][0]) for t in top]
    z = es[0]
    for e in es[1:]:
        z = z + e
    inv = 0.5 / z
    for k in range(K):
        ident = top[k][2]
        ii = jnp.floor(ident * (1.0 / NK))
        w_scr[k * H:(k + 1) * H, :] = es[k] * inv
        i_scr[k * H:(k + 1) * H, :] = ii
        j_scr[k * H:(k + 1) * H, :] = ident - ii * float(NK)

    wt_scr[...] = w_scr[...].T
    it_scr[...] = i_scr[...].T
    jt_scr[...] = j_scr[...].T

    sub = lax.broadcasted_iota(jnp.int32, (NK, PEER_HEADS * K), 0).astype(F32)

    def tok_body(t, _):
        wrow = wt_scr[pl.ds(t, 1), :]
        irow = it_scr[pl.ds(t, 1), :]
        jrow = jt_scr[pl.ds(t, 1), :]
        p1 = jnp.where(sub == irow, wrow, 0.0).astype(BF16)
        p2 = jnp.where(sub == jrow, 1.0, 0.0).astype(BF16)
        g = lax.dot_general(p1, p2, _NT, preferred_element_type=F32)
        for blk in range(NK // SUBLANES):
            gs_scr[pl.ds(blk * SUBLANES * PLANE + t, SUBLANES, stride=PLANE), :] = (
                g[blk * SUBLANES:(blk + 1) * SUBLANES, :])
        return 0

    lax.fori_loop(0, T, tok_body, 0, unroll=8)

    for i in range(NK):
        g_ref[:, i * NK:(i + 1) * NK] = gs_scr[i * PLANE:i * PLANE + T, :].astype(BF16)


def _route(n2, wq_bf, keys_bf):
    M, D = n2.shape
    Q = wq_bf.shape[1]
    T = ROUTE_T
    NK = PEER_KEYS
    HK = PEER_HEADS * PEER_TOPK
    return pl.pallas_call(
        _route_kernel,
        grid=(M // T,),
        in_specs=[
            pl.BlockSpec((T, D), lambda i: (i, 0)),
            pl.BlockSpec((D, Q), lambda i: (0, 0)),
            pl.BlockSpec(keys_bf.shape, lambda i: (0, 0, 0)),
        ],
        out_specs=pl.BlockSpec((T, NK * NK), lambda i: (i, 0)),
        out_shape=jax.ShapeDtypeStruct((M, NK * NK), BF16),
        scratch_shapes=[
            pltpu.VMEM((T, Q), BF16),
            pltpu.VMEM((2 * PEER_HEADS * NK, T), F32),
            pltpu.VMEM((2 * HK * SUBLANES, T), F32), pltpu.VMEM((2 * HK * SUBLANES, T), F32),
            pltpu.VMEM((HK, T), F32), pltpu.VMEM((HK, T), F32), pltpu.VMEM((HK, T), F32),
            pltpu.VMEM((T, HK), F32), pltpu.VMEM((T, HK), F32), pltpu.VMEM((T, HK), F32),
            pltpu.VMEM((NK * PLANE, NK), F32),
        ],
        compiler_params=_params(("parallel",)),
    )(n2, wq_bf, keys_bf)


def _expert_kernel(n_ref, u_ref, v_ref, g_ref, o_ref, *, sub):
    k = pl.program_id(1)

    @pl.when(k == 0)
    def _():
        o_ref[...] = jnp.zeros_like(o_ref)

    n = n_ref[...]
    te = u_ref.shape[0]
    acc = None
    for c in range(te // sub):
        cs = slice(c * sub, (c + 1) * sub)
        act = lax.dot_general(n, u_ref[cs, :], _NT, preferred_element_type=F32)
        a = (_twice_gelu(act) * g_ref[:, cs].astype(F32)).astype(BF16)
        part = jnp.dot(a, v_ref[cs, :], preferred_element_type=F32)
        acc = part if acc is None else acc + part
    o_ref[...] += acc


def _experts(n2, u_bf, v_bf, gates, tt, te, sub):
    M, D = n2.shape
    E = u_bf.shape[0]
    return pl.pallas_call(
        functools.partial(_expert_kernel, sub=sub),
        grid=(M // tt, E // te),
        in_specs=[
            pl.BlockSpec((tt, D), lambda i, k: (i, 0)),
            pl.BlockSpec((te, D), lambda i, k: (k, 0)),
            pl.BlockSpec((te, D), lambda i, k: (k, 0)),
            pl.BlockSpec((tt, te), lambda i, k: (i, k)),
        ],
        out_specs=pl.BlockSpec((tt, D), lambda i, k: (i, 0)),
        out_shape=jax.ShapeDtypeStruct((M, D), F32),
        compiler_params=_params(("parallel", "arbitrary")),
    )(n2, u_bf, v_bf, gates)


def _finish_kernel(h_ref, p_ref, g_ref, o_ref, *, final_norm):
    h2 = h_ref[...] + p_ref[...]
    if final_norm:
        ms = jnp.mean(h2 * h2, axis=-1, keepdims=True)
        h2 = h2 * lax.rsqrt(ms + EPS) * g_ref[...]
    o_ref[...] = h2


def _finish(h1, peer_out, g, tm, final_norm):
    M, D = h1.shape
    return pl.pallas_call(
        functools.partial(_finish_kernel, final_norm=final_norm),
        grid=(M // tm,),
        in_specs=[
            pl.BlockSpec((tm, D), lambda i: (i, 0)),
            pl.BlockSpec((tm, D), lambda i: (i, 0)),
            pl.BlockSpec((1, D), lambda i: (0, 0)),
        ],
        out_specs=pl.BlockSpec((tm, D), lambda i: (i, 0)),
        out_shape=jax.ShapeDtypeStruct((M, D), F32),
        compiler_params=_params(("parallel",)),
    )(h1, peer_out, g.reshape(1, D))


def _tile(n, want):
    t = min(n, want)
    assert n % t == 0, (n, t)
    return t


def kernel(x, mix_norm_g, w_in, conv_w, conv_b, lru_w_a, lru_b_a, lru_w_x, lru_b_x, lru_lambda,
           gmlp_ln_g, gmlp_ln_b, gmlp_w_s, gmlp_b_s, w_out, ffn_norm_g, peer_w_q, peer_sub_keys,
           peer_u, peer_v, final_norm_g):
    B, S, D = x.shape
    M = B * S
    depth = w_in.shape[0]
    assert S % CHUNK == 0 and M % ROUTE_T == 0
    h = x.reshape(M, D)
    for l in range(depth):
        z4 = _mix_in(h, mix_norm_g[l], w_in[l].astype(BF16), _tile(M, 512))
        cw = z4.shape[1] // 4
        hf, hb = _lru(z4.reshape(B, S, 4 * cw), conv_w[l], conv_b[l],
                      lru_w_a[l].astype(BF16), lru_b_a[l], lru_w_x[l].astype(BF16), lru_b_x[l],
                      lru_lambda[l], _tile(S, 256))
        bs_full = jnp.repeat(gmlp_b_s[l].T, cw // GMLP_HEADS, axis=1)
        b_out = _gmlp(z4, gmlp_ln_g[l], gmlp_ln_b[l], gmlp_w_s[l].astype(BF16), bs_full, _tile(M, 512))
        h1, n2 = _out_proj(hf.reshape(M, cw), hb.reshape(M, cw), z4, b_out, h,
                           w_out[l].astype(BF16), ffn_norm_g[l], _tile(M, 256))
        keys = peer_sub_keys[l].reshape(PEER_HEADS * 2, PEER_KEYS, -1).astype(BF16)
        gates = _route(n2, peer_w_q[l].astype(BF16), keys)
        peer_out = _experts(n2, peer_u[l].astype(BF16), peer_v[l].astype(BF16), gates,
                            _tile(M, 1024), 512, 256)
        h = _finish(h1, peer_out, final_norm_g, _tile(M, 512), l == depth - 1)
    return h.reshape(B, S, D)
```

```python
import functools

import jax
import jax.numpy as jnp
from jax import lax
from jax.experimental import pallas as pl
from jax.experimental.pallas import tpu as pltpu

F32 = jnp.float32
BF16 = jnp.bfloat16

EPS = 1e-6
LRU_C = 8.0
LRU_HEADS = 8
CONV_WIDTH = 4
CONV_LEFT = 2
GMLP_HEADS = 8
CHUNK = 128
PEER_HEADS = 8
PEER_KEYS = 128
PEER_TOPK = 16

LANES = 128
SUBLANES = 8
VMEM_LIMIT = 56 * 1024 * 1024

_NT = (((1,), (1,)), ((), ()))


def _gelu(x):
    c = 0.7978845608028654
    return 0.5 * x * (1.0 + jnp.tanh(c * (x + 0.044715 * (x * x * x))))


def _twice_gelu(x):
    c = 0.7978845608028654
    return x * (1.0 + jnp.tanh(x * (c + (c * 0.044715) * (x * x))))


def _sigmoid(x):
    return 0.5 * (1.0 + jnp.tanh(0.5 * x))


def _params(sem):
    return pltpu.CompilerParams(dimension_semantics=sem, vmem_limit_bytes=VMEM_LIMIT)


def _mix_in_kernel(x_ref, g_ref, w_ref, o_ref, n_scr):
    j = pl.program_id(1)

    @pl.when(j == 0)
    def _():
        x = x_ref[...]
        ms = jnp.mean(x * x, axis=-1, keepdims=True)
        n_scr[...] = (x * lax.rsqrt(ms + EPS) * g_ref[...]).astype(BF16)

    acc = jnp.dot(n_scr[...], w_ref[...], preferred_element_type=F32)

    @pl.when(j == 0)
    def _():
        o_ref[...] = acc

    @pl.when(j > 0)
    def _():
        o_ref[...] = _gelu(acc)


def _mix_in(x2, g, w_bf, tm):
    M, D = x2.shape
    N = w_bf.shape[1]
    tn = N // 4
    return pl.pallas_call(
        _mix_in_kernel,
        grid=(M // tm, 4),
        in_specs=[
            pl.BlockSpec((tm, D), lambda i, j: (i, 0)),
            pl.BlockSpec((1, D), lambda i, j: (0, 0)),
            pl.BlockSpec((D, tn), lambda i, j: (0, j)),
        ],
        out_specs=pl.BlockSpec((tm, tn), lambda i, j: (i, j)),
        out_shape=jax.ShapeDtypeStruct((M, N), F32),
        scratch_shapes=[pltpu.VMEM((tm, D), BF16)],
        compiler_params=_params(("parallel", "arbitrary")),
    )(x2, g.reshape(1, D), w_bf)


def _lru_kernel(xf_ref, xfp_ref, xfn_ref, xb_ref, xbp_ref, xbn_ref,
                cw_ref, cb_ref, wa_ref, ba_ref, wx_ref, bx_ref, lam_ref,
                hf_ref, hb_ref,
                a_scr, u_scr, cf_scr, cr_scr, *, tt, nt):
    j = pl.program_id(1)
    C = xf_ref.shape[-1]
    hd = C // LRU_HEADS

    @pl.when(j == 0)
    def _():
        cf_scr[...] = jnp.zeros_like(cf_scr)
        cr_scr[...] = jnp.zeros_like(cr_scr)

    def gates(cur_ref, prev_ref, next_ref, jt, d):
        xe = jnp.concatenate([jnp.where(jt > 0, prev_ref[0], 0.0), cur_ref[0],
                              jnp.where(jt < nt - 1, next_ref[0], 0.0)], axis=0)
        rows = tt + 2 * SUBLANES
        xc = cb_ref[...] + jnp.zeros((tt, C), F32)
        for k in range(CONV_WIDTH):
            shift = (CONV_LEFT - k) % rows
            xs = pltpu.roll(xe, shift, 0) if shift else xe
            xc = xc + xs[SUBLANES:SUBLANES + tt, :] * cw_ref[k:k + 1, :]
        xcb = xc.astype(BF16)
        ra = jnp.concatenate(
            [jnp.dot(xcb[:, h * hd:(h + 1) * hd], wa_ref[d, h], preferred_element_type=F32)
             for h in range(LRU_HEADS)], axis=1) + ba_ref[d:d + 1, :]
        rx = jnp.concatenate(
            [jnp.dot(xcb[:, h * hd:(h + 1) * hd], wx_ref[d, h], preferred_element_type=F32)
             for h in range(LRU_HEADS)], axis=1) + bx_ref[d:d + 1, :]
        r = _sigmoid(ra)
        i = _sigmoid(rx)
        nl = -lam_ref[d:d + 1, :]
        sp = jnp.maximum(nl, 0.0) + jnp.log(1.0 + jnp.exp(-jnp.abs(nl)))
        a = jnp.exp(-LRU_C * sp * r)
        a_scr[...] = a
        y = 1.0 - a * a
        u_scr[...] = jnp.where(y > 0.0, y * lax.rsqrt(y), 0.0) * (i * xc)

    row = lax.broadcasted_iota(jnp.int32, (SUBLANES, C), 0)
    ng = tt // SUBLANES

    gates(xf_ref, xfp_ref, xfn_ref, j, 0)

    def fwd_body(g, carry):
        r0 = pl.multiple_of(g * SUBLANES, SUBLANES)
        a = a_scr[pl.ds(r0, SUBLANES), :]
        u = u_scr[pl.ds(r0, SUBLANES), :]
        for s in (1, 2, 4):
            m = row >= s
            u = u + a * jnp.where(m, pltpu.roll(u, s, 0), 0.0)
            a = a * jnp.where(m, pltpu.roll(a, s, 0), 1.0)
        h = u + a * carry
        hf_ref[0, pl.ds(r0, SUBLANES), :] = h
        return h[SUBLANES - 1:SUBLANES, :]

    cf_scr[...] = lax.fori_loop(0, ng, fwd_body, cf_scr[...], unroll=4)

    gates(xb_ref, xbp_ref, xbn_ref, nt - 1 - j, 1)

    def bwd_body(gi, carry):
        g = ng - 1 - gi
        r0 = pl.multiple_of(g * SUBLANES, SUBLANES)
        a = a_scr[pl.ds(r0, SUBLANES), :]
        u = u_scr[pl.ds(r0, SUBLANES), :]
        for s in (1, 2, 4):
            m = row < SUBLANES - s
            u = u + a * jnp.where(m, pltpu.roll(u, SUBLANES - s, 0), 0.0)
            a = a * jnp.where(m, pltpu.roll(a, SUBLANES - s, 0), 1.0)
        h = u + a * carry
        hb_ref[0, pl.ds(r0, SUBLANES), :] = h
        return h[0:1, :]

    cr_scr[...] = lax.fori_loop(0, ng, bwd_body, cr_scr[...], unroll=4)


def _lru(z3, conv_w, conv_b, w_a_bf, b_a, w_x_bf, b_x, lam, tt):
    B, S, _ = z3.shape
    C = conv_w.shape[-1]
    nt = S // tt
    tb = tt // SUBLANES
    nb8 = S // SUBLANES

    def cur(jt):
        return lambda b, j: (b, jt(j), 0)

    def prev(jt):
        return lambda b, j: (b, jnp.maximum(jt(j) * tb - 1, 0), 0)

    def nxt(jt):
        return lambda b, j: (b, jnp.minimum((jt(j) + 1) * tb, nb8 - 1), 0)

    fw = lambda j: j
    bw = lambda j: nt - 1 - j
    full = lambda shape: pl.BlockSpec(shape, lambda b, j: (0,) * len(shape))
    return pl.pallas_call(
        functools.partial(_lru_kernel, tt=tt, nt=nt),
        grid=(B, nt),
        in_specs=[
            pl.BlockSpec((1, tt, C), cur(fw)),
            pl.BlockSpec((1, SUBLANES, C), prev(fw)),
            pl.BlockSpec((1, SUBLANES, C), nxt(fw)),
            pl.BlockSpec((1, tt, C), cur(bw)),
            pl.BlockSpec((1, SUBLANES, C), prev(bw)),
            pl.BlockSpec((1, SUBLANES, C), nxt(bw)),
            full((CONV_WIDTH, C)), full((1, C)),
            full(w_a_bf.shape), full((2, C)), full(w_x_bf.shape), full((2, C)), full((2, C)),
        ],
        out_specs=[
            pl.BlockSpec((1, tt, C), cur(fw)),
            pl.BlockSpec((1, tt, C), cur(bw)),
        ],
        out_shape=[jax.ShapeDtypeStruct((B, S, C), F32), jax.ShapeDtypeStruct((B, S, C), F32)],
        scratch_shapes=[
            pltpu.VMEM((tt, C), F32),
            pltpu.VMEM((tt, C), F32),
            pltpu.VMEM((1, C), F32),
            pltpu.VMEM((1, C), F32),
        ],
        compiler_params=_params(("parallel", "arbitrary")),
    )(z3, z3, z3, z3, z3, z3, conv_w, conv_b.reshape(1, C), w_a_bf, b_a, w_x_bf, b_x, lam)


def _gmlp_kernel(u_ref, v_ref, lng_ref, lnb_ref, ws_ref, bs_ref, o_ref, *, tc):
    v = v_ref[...]
    mu = jnp.mean(v, axis=-1, keepdims=True)
    vc = v - mu
    var = jnp.mean(vc * vc, axis=-1, keepdims=True)
    vb = (vc * lax.rsqrt(var + EPS) * lng_ref[...] + lnb_ref[...]).astype(BF16)
    hd = v.shape[-1] // GMLP_HEADS
    for c in range(tc // CHUNK):
        rows = slice(c * CHUNK, (c + 1) * CHUNK)
        mixed = jnp.concatenate(
            [jnp.dot(ws_ref[h], vb[rows, h * hd:(h + 1) * hd], preferred_element_type=F32)
             for h in range(GMLP_HEADS)], axis=1)
        o_ref[rows, :] = (u_ref[rows, :] * (mixed + bs_ref[...])).astype(BF16)


def _gmlp(z4, ln_g, ln_b, w_s_bf, bs_full, tc):
    M = z4.shape[0]
    C = ln_g.shape[-1]
    return pl.pallas_call(
        functools.partial(_gmlp_kernel, tc=tc),
        grid=(M // tc,),
        in_specs=[
            pl.BlockSpec((tc, C), lambda i: (i, 2)),
            pl.BlockSpec((tc, C), lambda i: (i, 3)),
            pl.BlockSpec((1, C), lambda i: (0, 0)),
            pl.BlockSpec((1, C), lambda i: (0, 0)),
            pl.BlockSpec(w_s_bf.shape, lambda i: (0, 0, 0)),
            pl.BlockSpec((CHUNK, C), lambda i: (0, 0)),
        ],
        out_specs=pl.BlockSpec((tc, C), lambda i: (i, 0)),
        out_shape=jax.ShapeDtypeStruct((M, C), BF16),
        compiler_params=_params(("parallel",)),
    )(z4, z4, ln_g.reshape(1, C), ln_b.reshape(1, C), w_s_bf, bs_full)


def _out_proj_kernel(hf_ref, hb_ref, gr_ref, bo_ref, x_ref, w_ref, g_ref, h1_ref, n2_ref):
    ca = hf_ref.shape[-1]
    a_out = ((hf_ref[...] + hb_ref[...]) * gr_ref[...]).astype(BF16)
    acc = jnp.dot(a_out, w_ref[0:ca, :], preferred_element_type=F32)
    acc = acc + jnp.dot(bo_ref[...], w_ref[ca:, :], preferred_element_type=F32)
    h1 = x_ref[...] + acc
    h1_ref[...] = h1
    ms = jnp.mean(h1 * h1, axis=-1, keepdims=True)
    n2_ref[...] = (h1 * lax.rsqrt(ms + EPS) * g_ref[...]).astype(BF16)


def _out_proj(hf, hb, z4, b_out, x2, w_bf, g, tm):
    M, D = x2.shape
    ca = hf.shape[-1]
    cb = b_out.shape[-1]
    return pl.pallas_call(
        _out_proj_kernel,
        grid=(M // tm,),
        in_specs=[
            pl.BlockSpec((tm, ca), lambda i: (i, 0)),
            pl.BlockSpec((tm, ca), lambda i: (i, 0)),
            pl.BlockSpec((tm, ca), lambda i: (i, 1)),
            pl.BlockSpec((tm, cb), lambda i: (i, 0)),
            pl.BlockSpec((tm, D), lambda i: (i, 0)),
            pl.BlockSpec(w_bf.shape, lambda i: (0, 0)),
            pl.BlockSpec((1, D), lambda i: (0, 0)),
        ],
        out_specs=[pl.BlockSpec((tm, D), lambda i: (i, 0)), pl.BlockSpec((tm, D), lambda i: (i, 0))],
        out_shape=[jax.ShapeDtypeStruct((M, D), F32), jax.ShapeDtypeStruct((M, D), BF16)],
        compiler_params=_params(("parallel",)),
    )(hf, hb, z4, b_out, x2, w_bf, g.reshape(1, D))


ROUTE_T = LANES
PLANE = ROUTE_T + SUBLANES


def _batcher_pairs(n):
    pairs = []

    def merge(lo, m, r):
        step = r * 2
        if step < m:
            merge(lo, m, step)
            merge(lo + r, m, step)
            pairs.extend((i, i + r) for i in range(lo + r, lo + m - r, step))
        else:
            pairs.append((lo, lo + r))

    def sort(lo, m):
        if m > 1:
            sort(lo, m // 2)
            sort(lo + m // 2, m // 2)
            merge(lo, m, 1)

    sort(0, n)
    return pairs


def _beats(x, y):
    return (x[0] > y[0]) | ((x[0] == y[0]) & (x[1] < y[1]))


def _best(x, y):
    g = _beats(x, y)
    return (jnp.maximum(x[0], y[0]),) + tuple(jnp.where(g, p, q) for p, q in zip(x[1:], y[1:]))


def _compare_exchange(x, y):
    g = _beats(x, y)
    hi = (jnp.maximum(x[0], y[0]),) + tuple(jnp.where(g, p, q) for p, q in zip(x[1:], y[1:]))
    lo = (jnp.minimum(x[0], y[0]),) + tuple(jnp.where(g, q, p) for p, q in zip(x[1:], y[1:]))
    return hi, lo


def _sort_desc(items):
    items = list(items)
    for i, j in _batcher_pairs(len(items)):
        items[i], items[j] = _compare_exchange(items[i], items[j])
    return items


def _bitonic_merge(items):
    items = list(items)
    n = len(items)
    d = n // 2
    while d >= 1:
        for i in range(n):
            if i & d == 0:
                items[i], items[i + d] = _compare_exchange(items[i], items[i + d])
        d //= 2
    return items


def _merge_top(x, y):
    n = len(x)
    return _bitonic_merge([_best(x[k], y[n - 1 - k]) for k in range(n)])


def _merge_full(x, y):
    return _bitonic_merge(list(x) + list(reversed(y)))


def _route_kernel(n_ref, wq_ref, keys_ref, g_ref, q_scr, s_scr, sv_scr, si_scr, w_scr, i_scr, j_scr,
                  wt_scr, it_scr, jt_scr, gs_scr):
    T = ROUTE_T
    K = PEER_TOPK
    NK = PEER_KEYS
    q_scr[...] = jnp.dot(n_ref[...], wq_ref[...], preferred_element_type=F32).astype(BF16)

    H = PEER_HEADS
    row8 = lax.broadcasted_iota(jnp.int32, (SUBLANES, T), 0).astype(F32)

    for u in range(2 * H):
        s_scr[u * NK:(u + 1) * NK, :] = lax.dot_general(
            keys_ref[u], q_scr[:, u * NK:(u + 1) * NK], _NT, preferred_element_type=F32)

    def unit_body(u, _):
        c0 = pl.multiple_of(u * NK, NK)
        items = [(s_scr[pl.ds(c0 + b * SUBLANES, SUBLANES), :], row8 + float(b * SUBLANES))
                 for b in range(NK // SUBLANES)]
        items = _sort_desc(items)
        half = jnp.bitwise_and(u, 1)
        head = jnp.right_shift(u, 1)
        for k, (v, i) in enumerate(items):
            r0 = pl.multiple_of(((half * K + k) * H + head) * SUBLANES, SUBLANES)
            sv_scr[pl.ds(r0, SUBLANES), :] = v
            si_scr[pl.ds(r0, SUBLANES), :] = i
        return 0

    lax.fori_loop(0, 2 * H, unit_body, 0)

    def half_top(half):
        def residue_list(r):
            starts = [(half * K + k) * H * SUBLANES + r for k in range(K)]
            return (tuple(sv_scr[pl.ds(st, H, stride=SUBLANES), :] for st in starts),
                    tuple(si_scr[pl.ds(st, H, stride=SUBLANES), :] for st in starts))

        def merge_body(r, acc):
            new = residue_list(r)
            out = _merge_top(list(zip(*acc)), list(zip(*new)))
            return tuple(x[0] for x in out), tuple(x[1] for x in out)

        vals, idxs = lax.fori_loop(1, SUBLANES, merge_body, residue_list(0))
        return list(zip(vals, idxs))

    a = half_top(0)
    b = half_top(1)

    def pair(p, r):
        code = jnp.full((H, T), float(p * K + r), F32)
        return (a[p][0] + b[r][0], code, a[p][1] * float(NK) + b[r][1])

    rows = [[pair(p, r) for r in range(K // (p + 1))] for p in range(K)]
    singles = [rows[p][0] for p in range(8, K)]
    l1 = _merge_full(rows[1], singles)
    l2 = _sort_desc(rows[2] + rows[3] + rows[4] + rows[5] + rows[6])
    top = _merge_top(rows[0], l1)
    top = _merge_top(top, l2)
    top[K - 2] = _best(top[K - 2], rows[7][1])
    top[K - 1] = _best(top[K - 1], rows[7][0])
    top = _bitonic_merge(top)

    es = [jnp.exp(t[0] - top[0][0]) for t in top]
    z = es[0]
    for e in es[1:]:
        z = z + e
    inv = 0.5 / z
    for k in range(K):
        ident = top[k][2]
        ii = jnp.floor(ident * (1.0 / NK))
        w_scr[k * H:(k + 1) * H, :] = es[k] * inv
        i_scr[k * H:(k + 1) * H, :] = ii
        j_scr[k * H:(k + 1) * H, :] = ident - ii * float(NK)

    wt_scr[...] = w_scr[...].T
    it_scr[...] = i_scr[...].T
    jt_scr[...] = j_scr[...].T

    sub = lax.broadcasted_iota(jnp.int32, (NK, PEER_HEADS * K), 0).astype(F32)

    def tok_body(t, _):
        wrow = wt_scr[pl.ds(t, 1), :]
        irow = it_scr[pl.ds(t, 1), :]
        jrow = jt_scr[pl.ds(t, 1), :]
        p1 = jnp.where(sub == irow, wrow, 0.0).astype(BF16)
        p2 = jnp.where(sub == jrow, 1.0, 0.0).astype(BF16)
        g = lax.dot_general(p1, p2, _NT, preferred_element_type=F32)
        for blk in range(NK // SUBLANES):
            gs_scr[pl.ds(blk * SUBLANES * PLANE + t, SUBLANES, stride=PLANE), :] = (
                g[blk * SUBLANES:(blk + 1) * SUBLANES, :])
        return 0

    lax.fori_loop(0, T, tok_body, 0, unroll=8)

    for i in range(NK):
        g_ref[:, i * NK:(i + 1) * NK] = gs_scr[i * PLANE:i * PLANE + T, :].astype(BF16)


def _route(n2, wq_bf, keys_bf):
    M, D = n2.shape
    Q = wq_bf.shape[1]
    T = ROUTE_T
    NK = PEER_KEYS
    HK = PEER_HEADS * PEER_TOPK
    return pl.pallas_call(
        _route_kernel,
        grid=(M // T,),
        in_specs=[
            pl.BlockSpec((T, D), lambda i: (i, 0)),
            pl.BlockSpec((D, Q), lambda i: (0, 0)),
            pl.BlockSpec(keys_bf.shape, lambda i: (0, 0, 0)),
        ],
        out_specs=pl.BlockSpec((T, NK * NK), lambda i: (i, 0)),
        out_shape=jax.ShapeDtypeStruct((M, NK * NK), BF16),
        scratch_shapes=[
            pltpu.VMEM((T, Q), BF16),
            pltpu.VMEM((2 * PEER_HEADS * NK, T), F32),
            pltpu.VMEM((2 * HK * SUBLANES, T), F32), pltpu.VMEM((2 * HK * SUBLANES, T), F32),
            pltpu.VMEM((HK, T), F32), pltpu.VMEM((HK, T), F32), pltpu.VMEM((HK, T), F32),
            pltpu.VMEM((T, HK), F32), pltpu.VMEM((T, HK), F32), pltpu.VMEM((T, HK), F32),
            pltpu.VMEM((NK * PLANE, NK), F32),
        ],
        compiler_params=_params(("parallel",)),
    )(n2, wq_bf, keys_bf)


def _expert_kernel(n_ref, u_ref, v_ref, g_ref, o_ref, *, sub):
    k = pl.program_id(1)

    @pl.when(k == 0)
    def _():
        o_ref[...] = jnp.zeros_like(o_ref)

    n = n_ref[...]
    te = u_ref.shape[0]
    acc = None
    for c in range(te // sub):
        cs = slice(c * sub, (c + 1) * sub)
        act = lax.dot_general(n, u_ref[cs, :], _NT, preferred_element_type=F32)
        a = _twice_gelu(act.astype(BF16)) * g_ref[:, cs]
        part = jnp.dot(a, v_ref[cs, :], preferred_element_type=F32)
        acc = part if acc is None else acc + part
    o_ref[...] += acc


def _experts(n2, u_bf, v_bf, gates, tt, te, sub):
    M, D = n2.shape
    E = u_bf.shape[0]
    return pl.pallas_call(
        functools.partial(_expert_kernel, sub=sub),
        grid=(M // tt, E // te),
        in_specs=[
            pl.BlockSpec((tt, D), lambda i, k: (i, 0)),
            pl.BlockSpec((te, D), lambda i, k: (k, 0)),
            pl.BlockSpec((te, D), lambda i, k: (k, 0)),
            pl.BlockSpec((tt, te), lambda i, k: (i, k)),
        ],
        out_specs=pl.BlockSpec((tt, D), lambda i, k: (i, 0)),
        out_shape=jax.ShapeDtypeStruct((M, D), F32),
        compiler_params=_params(("parallel", "arbitrary")),
    )(n2, u_bf, v_bf, gates)


def _finish_kernel(h_ref, p_ref, g_ref, o_ref, *, final_norm):
    h2 = h_ref[...] + p_ref[...]
    if final_norm:
        ms = jnp.mean(h2 * h2, axis=-1, keepdims=True)
        h2 = h2 * lax.rsqrt(ms + EPS) * g_ref[...]
    o_ref[...] = h2


def _finish(h1, peer_out, g, tm, final_norm):
    M, D = h1.shape
    return pl.pallas_call(
        functools.partial(_finish_kernel, final_norm=final_norm),
        grid=(M // tm,),
        in_specs=[
            pl.BlockSpec((tm, D), lambda i: (i, 0)),
            pl.BlockSpec((tm, D), lambda i: (i, 0)),
            pl.BlockSpec((1, D), lambda i: (0, 0)),
        ],
        out_specs=pl.BlockSpec((tm, D), lambda i: (i, 0)),
        out_shape=jax.ShapeDtypeStruct((M, D), F32),
        compiler_params=_params(("parallel",)),
    )(h1, peer_out, g.reshape(1, D))


def _tile(n, want):
    t = min(n, want)
    assert n % t == 0, (n, t)
    return t


def kernel(x, mix_norm_g, w_in, conv_w, conv_b, lru_w_a, lru_b_a, lru_w_x, lru_b_x, lru_lambda,
           gmlp_ln_g, gmlp_ln_b, gmlp_w_s, gmlp_b_s, w_out, ffn_norm_g, peer_w_q, peer_sub_keys,
           peer_u, peer_v, final_norm_g):
    B, S, D = x.shape
    M = B * S
    depth = w_in.shape[0]
    assert S % CHUNK == 0 and M % ROUTE_T == 0
    h = x.reshape(M, D)
    for l in range(depth):
        z4 = _mix_in(h, mix_norm_g[l], w_in[l].astype(BF16), _tile(M, 512))
        cw = z4.shape[1] // 4
        hf, hb = _lru(z4.reshape(B, S, 4 * cw), conv_w[l], conv_b[l],
                      lru_w_a[l].astype(BF16), lru_b_a[l], lru_w_x[l].astype(BF16), lru_b_x[l],
                      lru_lambda[l], _tile(S, 256))
        bs_full = jnp.repeat(gmlp_b_s[l].T, cw // GMLP_HEADS, axis=1)
        b_out = _gmlp(z4, gmlp_ln_g[l], gmlp_ln_b[l], gmlp_w_s[l].astype(BF16), bs_full, _tile(M, 512))
        h1, n2 = _out_proj(hf.reshape(M, cw), hb.reshape(M, cw), z4, b_out, h,
                           w_out[l].astype(BF16), ffn_norm_g[l], _tile(M, 256))
        keys = peer_sub_keys[l].reshape(PEER_HEADS * 2, PEER_KEYS, -1).astype(BF16)
        gates = _route(n2, peer_w_q[l].astype(BF16), keys)
        peer_out = _experts(n2, peer_u[l].astype(BF16), peer_v[l].astype(BF16), gates,
                            _tile(M, 1024), 512, 256)
        h = _finish(h1, peer_out, final_norm_g, _tile(M, 512), l == depth - 1)
    return h.reshape(B, S, D)
```

```python
import functools

import jax
import jax.numpy as jnp
from jax import lax
from jax.experimental import pallas as pl
from jax.experimental.pallas import tpu as pltpu

F32 = jnp.float32
BF16 = jnp.bfloat16

EPS = 1e-6
LRU_C = 8.0
LRU_HEADS = 8
CONV_WIDTH = 4
CONV_LEFT = 2
GMLP_HEADS = 8
CHUNK = 128
PEER_HEADS = 8
PEER_KEYS = 128
PEER_TOPK = 16

LANES = 128
SUBLANES = 8
VMEM_LIMIT = 56 * 1024 * 1024

_NT = (((1,), (1,)), ((), ()))


def _gelu(x):
    c = 0.7978845608028654
    return 0.5 * x * (1.0 + jnp.tanh(c * (x + 0.044715 * (x * x * x))))


def _twice_gelu(x):
    c = 0.7978845608028654
    return x * (1.0 + jnp.tanh(x * (c + (c * 0.044715) * (x * x))))


def _sigmoid(x):
    return 0.5 * (1.0 + jnp.tanh(0.5 * x))


def _params(sem):
    return pltpu.CompilerParams(dimension_semantics=sem, vmem_limit_bytes=VMEM_LIMIT)


def _mix_in_kernel(x_ref, g_ref, w_ref, o_ref, n_scr):
    j = pl.program_id(1)

    @pl.when(j == 0)
    def _():
        x = x_ref[...]
        ms = jnp.mean(x * x, axis=-1, keepdims=True)
        n_scr[...] = (x * lax.rsqrt(ms + EPS) * g_ref[...]).astype(BF16)

    acc = jnp.dot(n_scr[...], w_ref[...], preferred_element_type=F32)

    @pl.when(j == 0)
    def _():
        o_ref[...] = acc

    @pl.when(j > 0)
    def _():
        o_ref[...] = _gelu(acc)


def _mix_in(x2, g, w_bf, tm):
    M, D = x2.shape
    N = w_bf.shape[1]
    tn = N // 4
    return pl.pallas_call(
        _mix_in_kernel,
        grid=(M // tm, 4),
        in_specs=[
            pl.BlockSpec((tm, D), lambda i, j: (i, 0)),
            pl.BlockSpec((1, D), lambda i, j: (0, 0)),
            pl.BlockSpec((D, tn), lambda i, j: (0, j)),
        ],
        out_specs=pl.BlockSpec((tm, tn), lambda i, j: (i, j)),
        out_shape=jax.ShapeDtypeStruct((M, N), F32),
        scratch_shapes=[pltpu.VMEM((tm, D), BF16)],
        compiler_params=_params(("parallel", "arbitrary")),
    )(x2, g.reshape(1, D), w_bf)


def _lru_kernel(xf_ref, xfp_ref, xfn_ref, xb_ref, xbp_ref, xbn_ref,
                cw_ref, cb_ref, wa_ref, ba_ref, wx_ref, bx_ref, lam_ref,
                hf_ref, hb_ref,
                a_scr, u_scr, cf_scr, cr_scr, *, tt, nt):
    j = pl.program_id(1)
    C = xf_ref.shape[-1]
    hd = C // LRU_HEADS

    @pl.when(j == 0)
    def _():
        cf_scr[...] = jnp.zeros_like(cf_scr)
        cr_scr[...] = jnp.zeros_like(cr_scr)

    def gates(cur_ref, prev_ref, next_ref, jt, d):
        xe = jnp.concatenate([jnp.where(jt > 0, prev_ref[0], 0.0), cur_ref[0],
                              jnp.where(jt < nt - 1, next_ref[0], 0.0)], axis=0)
        rows = tt + 2 * SUBLANES
        xc = cb_ref[...] + jnp.zeros((tt, C), F32)
        for k in range(CONV_WIDTH):
            shift = (CONV_LEFT - k) % rows
            xs = pltpu.roll(xe, shift, 0) if shift else xe
            xc = xc + xs[SUBLANES:SUBLANES + tt, :] * cw_ref[k:k + 1, :]
        xcb = xc.astype(BF16)
        ra = jnp.concatenate(
            [jnp.dot(xcb[:, h * hd:(h + 1) * hd], wa_ref[d, h], preferred_element_type=F32)
             for h in range(LRU_HEADS)], axis=1) + ba_ref[d:d + 1, :]
        rx = jnp.concatenate(
            [jnp.dot(xcb[:, h * hd:(h + 1) * hd], wx_ref[d, h], preferred_element_type=F32)
             for h in range(LRU_HEADS)], axis=1) + bx_ref[d:d + 1, :]
        r = _sigmoid(ra)
        i = _sigmoid(rx)
        nl = -lam_ref[d:d + 1, :]
        sp = jnp.maximum(nl, 0.0) + jnp.log(1.0 + jnp.exp(-jnp.abs(nl)))
        a = jnp.exp(-LRU_C * sp * r)
        a_scr[...] = a
        y = 1.0 - a * a
        u_scr[...] = jnp.where(y > 0.0, y * lax.rsqrt(y), 0.0) * (i * xc)

    row = lax.broadcasted_iota(jnp.int32, (SUBLANES, C), 0)
    ng = tt // SUBLANES

    gates(xf_ref, xfp_ref, xfn_ref, j, 0)

    def fwd_body(g, carry):
        r0 = pl.multiple_of(g * SUBLANES, SUBLANES)
        a = a_scr[pl.ds(r0, SUBLANES), :]
        u = u_scr[pl.ds(r0, SUBLANES), :]
        for s in (1, 2, 4):
            m = row >= s
            u = u + a * jnp.where(m, pltpu.roll(u, s, 0), 0.0)
            a = a * jnp.where(m, pltpu.roll(a, s, 0), 1.0)
        h = u + a * carry
        hf_ref[0, pl.ds(r0, SUBLANES), :] = h
        return h[SUBLANES - 1:SUBLANES, :]

    cf_scr[...] = lax.fori_loop(0, ng, fwd_body, cf_scr[...], unroll=4)

    gates(xb_ref, xbp_ref, xbn_ref, nt - 1 - j, 1)

    def bwd_body(gi, carry):
        g = ng - 1 - gi
        r0 = pl.multiple_of(g * SUBLANES, SUBLANES)
        a = a_scr[pl.ds(r0, SUBLANES), :]
        u = u_scr[pl.ds(r0, SUBLANES), :]
        for s in (1, 2, 4):
            m = row < SUBLANES - s
            u = u + a * jnp.where(m, pltpu.roll(u, SUBLANES - s, 0), 0.0)
            a = a * jnp.where(m, pltpu.roll(a, SUBLANES - s, 0), 1.0)
        h = u + a * carry
        hb_ref[0, pl.ds(r0, SUBLANES), :] = h
        return h[0:1, :]

    cr_scr[...] = lax.fori_loop(0, ng, bwd_body, cr_scr[...], unroll=4)


def _lru(z3, conv_w, conv_b, w_a_bf, b_a, w_x_bf, b_x, lam, tt):
    B, S, _ = z3.shape
    C = conv_w.shape[-1]
    nt = S // tt
    tb = tt // SUBLANES
    nb8 = S // SUBLANES

    def cur(jt):
        return lambda b, j: (b, jt(j), 0)

    def prev(jt):
        return lambda b, j: (b, jnp.maximum(jt(j) * tb - 1, 0), 0)

    def nxt(jt):
        return lambda b, j: (b, jnp.minimum((jt(j) + 1) * tb, nb8 - 1), 0)

    fw = lambda j: j
    bw = lambda j: nt - 1 - j
    full = lambda shape: pl.BlockSpec(shape, lambda b, j: (0,) * len(shape))
    return pl.pallas_call(
        functools.partial(_lru_kernel, tt=tt, nt=nt),
        grid=(B, nt),
        in_specs=[
            pl.BlockSpec((1, tt, C), cur(fw)),
            pl.BlockSpec((1, SUBLANES, C), prev(fw)),
            pl.BlockSpec((1, SUBLANES, C), nxt(fw)),
            pl.BlockSpec((1, tt, C), cur(bw)),
            pl.BlockSpec((1, SUBLANES, C), prev(bw)),
            pl.BlockSpec((1, SUBLANES, C), nxt(bw)),
            full((CONV_WIDTH, C)), full((1, C)),
            full(w_a_bf.shape), full((2, C)), full(w_x_bf.shape), full((2, C)), full((2, C)),
        ],
        out_specs=[
            pl.BlockSpec((1, tt, C), cur(fw)),
            pl.BlockSpec((1, tt, C), cur(bw)),
        ],
        out_shape=[jax.ShapeDtypeStruct((B, S, C), F32), jax.ShapeDtypeStruct((B, S, C), F32)],
        scratch_shapes=[
            pltpu.VMEM((tt, C), F32),
            pltpu.VMEM((tt, C), F32),
            pltpu.VMEM((1, C), F32),
            pltpu.VMEM((1, C), F32),
        ],
        compiler_params=_params(("parallel", "arbitrary")),
    )(z3, z3, z3, z3, z3, z3, conv_w, conv_b.reshape(1, C), w_a_bf, b_a, w_x_bf, b_x, lam)


def _gmlp_kernel(u_ref, v_ref, lng_ref, lnb_ref, ws_ref, bs_ref, o_ref, *, tc):
    v = v_ref[...]
    mu = jnp.mean(v, axis=-1, keepdims=True)
    vc = v - mu
    var = jnp.mean(vc * vc, axis=-1, keepdims=True)
    vb = (vc * lax.rsqrt(var + EPS) * lng_ref[...] + lnb_ref[...]).astype(BF16)
    hd = v.shape[-1] // GMLP_HEADS
    for c in range(tc // CHUNK):
        rows = slice(c * CHUNK, (c + 1) * CHUNK)
        mixed = jnp.concatenate(
            [jnp.dot(ws_ref[h], vb[rows, h * hd:(h + 1) * hd], preferred_element_type=F32)
             for h in range(GMLP_HEADS)], axis=1)
        o_ref[rows, :] = (u_ref[rows, :] * (mixed + bs_ref[...])).astype(BF16)


def _gmlp(z4, ln_g, ln_b, w_s_bf, bs_full, tc):
    M = z4.shape[0]
    C = ln_g.shape[-1]
    return pl.pallas_call(
        functools.partial(_gmlp_kernel, tc=tc),
        grid=(M // tc,),
        in_specs=[
            pl.BlockSpec((tc, C), lambda i: (i, 2)),
            pl.BlockSpec((tc, C), lambda i: (i, 3)),
            pl.BlockSpec((1, C), lambda i: (0, 0)),
            pl.BlockSpec((1, C), lambda i: (0, 0)),
            pl.BlockSpec(w_s_bf.shape, lambda i: (0, 0, 0)),
            pl.BlockSpec((CHUNK, C), lambda i: (0, 0)),
        ],
        out_specs=pl.BlockSpec((tc, C), lambda i: (i, 0)),
        out_shape=jax.ShapeDtypeStruct((M, C), BF16),
        compiler_params=_params(("parallel",)),
    )(z4, z4, ln_g.reshape(1, C), ln_b.reshape(1, C), w_s_bf, bs_full)


def _out_proj_kernel(hf_ref, hb_ref, gr_ref, bo_ref, x_ref, w_ref, g_ref, h1_ref, n2_ref):
    ca = hf_ref.shape[-1]
    a_out = ((hf_ref[...] + hb_ref[...]) * gr_ref[...]).astype(BF16)
    acc = jnp.dot(a_out, w_ref[0:ca, :], preferred_element_type=F32)
    acc = acc + jnp.dot(bo_ref[...], w_ref[ca:, :], preferred_element_type=F32)
    h1 = x_ref[...] + acc
    h1_ref[...] = h1
    ms = jnp.mean(h1 * h1, axis=-1, keepdims=True)
    n2_ref[...] = (h1 * lax.rsqrt(ms + EPS) * g_ref[...]).astype(BF16)


def _out_proj(hf, hb, z4, b_out, x2, w_bf, g, tm):
    M, D = x2.shape
    ca = hf.shape[-1]
    cb = b_out.shape[-1]
    return pl.pallas_call(
        _out_proj_kernel,
        grid=(M // tm,),
        in_specs=[
            pl.BlockSpec((tm, ca), lambda i: (i, 0)),
            pl.BlockSpec((tm, ca), lambda i: (i, 0)),
            pl.BlockSpec((tm, ca), lambda i: (i, 1)),
            pl.BlockSpec((tm, cb), lambda i: (i, 0)),
            pl.BlockSpec((tm, D), lambda i: (i, 0)),
            pl.BlockSpec(w_bf.shape, lambda i: (0, 0)),
            pl.BlockSpec((1, D), lambda i: (0, 0)),
        ],
        out_specs=[pl.BlockSpec((tm, D), lambda i: (i, 0)), pl.BlockSpec((tm, D), lambda i: (i, 0))],
        out_shape=[jax.ShapeDtypeStruct((M, D), F32), jax.ShapeDtypeStruct((M, D), BF16)],
        compiler_params=_params(("parallel",)),
    )(hf, hb, z4, b_out, x2, w_bf, g.reshape(1, D))


ROUTE_T = LANES
PLANE = ROUTE_T + SUBLANES


def _batcher_pairs(n):
    pairs = []

    def merge(lo, m, r):
        step = r * 2
        if step < m:
            merge(lo, m, step)
            merge(lo + r, m, step)
            pairs.extend((i, i + r) for i in range(lo + r, lo + m - r, step))
        else:
            pairs.append((lo, lo + r))

    def sort(lo, m):
        if m > 1:
            sort(lo, m // 2)
            sort(lo + m // 2, m // 2)
            merge(lo, m, 1)

    sort(0, n)
    return pairs


def _beats(x, y):
    return (x[0] > y[0]) | ((x[0] == y[0]) & (x[1] < y[1]))


def _best(x, y):
    g = _beats(x, y)
    return (jnp.maximum(x[0], y[0]),) + tuple(jnp.where(g, p, q) for p, q in zip(x[1:], y[1:]))


def _compare_exchange(x, y):
    g = _beats(x, y)
    hi = (jnp.maximum(x[0], y[0]),) + tuple(jnp.where(g, p, q) for p, q in zip(x[1:], y[1:]))
    lo = (jnp.minimum(x[0], y[0]),) + tuple(jnp.where(g, q, p) for p, q in zip(x[1:], y[1:]))
    return hi, lo


def _sort_desc(items):
    items = list(items)
    for i, j in _batcher_pairs(len(items)):
        items[i], items[j] = _compare_exchange(items[i], items[j])
    return items


def _bitonic_merge(items):
    items = list(items)
    n = len(items)
    d = n // 2
    while d >= 1:
        for i in range(n):
            if i & d == 0:
                items[i], items[i + d] = _compare_exchange(items[i], items[i + d])
        d //= 2
    return items


def _merge_top(x, y):
    n = len(x)
    return _bitonic_merge([_best(x[k], y[n - 1 - k]) for k in range(n)])


def _merge_full(x, y):
    return _bitonic_merge(list(x) + list(reversed(y)))


def _route_kernel(n_ref, wq_ref, keys_ref, g_ref, q_scr, s_scr, sv_scr, si_scr, w_scr, i_scr, j_scr,
                  wt_scr, it_scr, jt_scr, gs_scr):
    T = ROUTE_T
    K = PEER_TOPK
    NK = PEER_KEYS
    q_scr[...] = jnp.dot(n_ref[...], wq_ref[...], preferred_element_type=F32).astype(BF16)

    H = PEER_HEADS
    row8 = lax.broadcasted_iota(jnp.int32, (SUBLANES, T), 0).astype(F32)

    for u in range(2 * H):
        s_scr[u * NK:(u + 1) * NK, :] = lax.dot_general(
            keys_ref[u], q_scr[:, u * NK:(u + 1) * NK], _NT, preferred_element_type=F32)

    def unit_body(u, _):
        c0 = pl.multiple_of(u * NK, NK)
        items = [(s_scr[pl.ds(c0 + b * SUBLANES, SUBLANES), :], row8 + float(b * SUBLANES))
                 for b in range(NK // SUBLANES)]
        items = _sort_desc(items)
        half = jnp.bitwise_and(u, 1)
        head = jnp.right_shift(u, 1)
        for k, (v, i) in enumerate(items):
            r0 = pl.multiple_of(((half * K + k) * H + head) * SUBLANES, SUBLANES)
            sv_scr[pl.ds(r0, SUBLANES), :] = v
            si_scr[pl.ds(r0, SUBLANES), :] = i
        return 0

    lax.fori_loop(0, 2 * H, unit_body, 0)

    def half_top(half):
        def residue_list(r):
            starts = [(half * K + k) * H * SUBLANES + r for k in range(K)]
            return (tuple(sv_scr[pl.ds(st, H, stride=SUBLANES), :] for st in starts),
                    tuple(si_scr[pl.ds(st, H, stride=SUBLANES), :] for st in starts))

        def merge_body(r, acc):
            new = residue_list(r)
            out = _merge_top(list(zip(*acc)), list(zip(*new)))
            return tuple(x[0] for x in out), tuple(x[1] for x in out)

        vals, idxs = lax.fori_loop(1, SUBLANES, merge_body, residue_list(0))
        return list(zip(vals, idxs))

    a = half_top(0)
    b = half_top(1)

    def pair(p, r):
        code = jnp.full((H, T), float(p * K + r), F32)
        return (a[p][0] + b[r][0], code, a[p][1] * float(NK) + b[r][1])

    rows = [[pair(p, r) for r in range(K // (p + 1))] for p in range(K)]
    singles = [rows[p][0] for p in range(8, K)]
    l1 = _merge_full(rows[1], singles)
    l2 = _sort_desc(rows[2] + rows[3] + rows[4] + rows[5] + rows[6])
    top = _merge_top(rows[0], l1)
    top = _merge_top(top, l2)
    top[K - 2] = _best(top[K - 2], rows[7][1])
    top[K - 1] = _best(top[K - 1], rows[7][0])
    top = _bitonic_merge(top)

    es = [jnp.exp(t[0] - top[0][0]) for t in top]
    z = es[0]
    for e in es[1:]:
        z = z + e
    inv = 0.5 / z
    for k in range(K):
        ident = top[k][2]
        ii = jnp.floor(ident * (1.0 / NK))
        w_scr[k * H:(k + 1) * H, :] = es[k] * inv
        i_scr[k * H:(k + 1) * H, :] = ii
        j_scr[k * H:(k + 1) * H, :] = ident - ii * float(NK)

    wt_scr[...] = w_scr[...].T
    it_scr[...] = i_scr[...].T
    jt_scr[...] = j_scr[...].T

    sub = lax.broadcasted_iota(jnp.int32, (NK, PEER_HEADS * K), 0).astype(F32)

    zero_blk = jnp.zeros((PEER_HEADS * K, NK), BF16)

    def factors(t):
        wrow = wt_scr[pl.ds(t, 1), :]
        irow = it_scr[pl.ds(t, 1), :]
        jrow = jt_scr[pl.ds(t, 1), :]
        p1 = jnp.where(sub == irow, wrow, 0.0).astype(BF16)
        p2 = jnp.where(sub == jrow, 1.0, 0.0).astype(BF16).T
        return p1, p2

    def pair_body(q, _):
        t0 = 2 * q
        p1a, p2a = factors(t0)
        p1b, p2b = factors(t0 + 1)
        lhs = jnp.concatenate([p1a, p1b], axis=1)
        rhs = jnp.concatenate([jnp.concatenate([p2a, zero_blk], axis=1),
                               jnp.concatenate([zero_blk, p2b], axis=1)], axis=0)
        g = jnp.dot(lhs, rhs, preferred_element_type=F32)
        for blk in range(NK // SUBLANES):
            rows = slice(blk * SUBLANES, (blk + 1) * SUBLANES)
            base = blk * SUBLANES * PLANE + t0
            gs_scr[pl.ds(base, SUBLANES, stride=PLANE), :] = g[rows, 0:NK]
            gs_scr[pl.ds(base + 1, SUBLANES, stride=PLANE), :] = g[rows, NK:2 * NK]
        return 0

    lax.fori_loop(0, T // 2, pair_body, 0, unroll=True)

    for i in range(NK):
        g_ref[:, i * NK:(i + 1) * NK] = gs_scr[i * PLANE:i * PLANE + T, :].astype(BF16)


def _route(n2, wq_bf, keys_bf):
    M, D = n2.shape
    Q = wq_bf.shape[1]
    T = ROUTE_T
    NK = PEER_KEYS
    HK = PEER_HEADS * PEER_TOPK
    return pl.pallas_call(
        _route_kernel,
        grid=(M // T,),
        in_specs=[
            pl.BlockSpec((T, D), lambda i: (i, 0)),
            pl.BlockSpec((D, Q), lambda i: (0, 0)),
            pl.BlockSpec(keys_bf.shape, lambda i: (0, 0, 0)),
        ],
        out_specs=pl.BlockSpec((T, NK * NK), lambda i: (i, 0)),
        out_shape=jax.ShapeDtypeStruct((M, NK * NK), BF16),
        scratch_shapes=[
            pltpu.VMEM((T, Q), BF16),
            pltpu.VMEM((2 * PEER_HEADS * NK, T), F32),
            pltpu.VMEM((2 * HK * SUBLANES, T), F32), pltpu.VMEM((2 * HK * SUBLANES, T), F32),
            pltpu.VMEM((HK, T), F32), pltpu.VMEM((HK, T), F32), pltpu.VMEM((HK, T), F32),
            pltpu.VMEM((T, HK), F32), pltpu.VMEM((T, HK), F32), pltpu.VMEM((T, HK), F32),
            pltpu.VMEM((NK * PLANE, NK), F32),
        ],
        compiler_params=_params(("parallel",)),
    )(n2, wq_bf, keys_bf)


def _expert_kernel(n_ref, u_ref, v_ref, g_ref, o_ref, *, sub):
    k = pl.program_id(1)

    @pl.when(k == 0)
    def _():
        o_ref[...] = jnp.zeros_like(o_ref)

    n = n_ref[...]
    te = u_ref.shape[0]
    acc = None
    for c in range(te // sub):
        cs = slice(c * sub, (c + 1) * sub)
        act = lax.dot_general(n, u_ref[cs, :], _NT, preferred_element_type=F32)
        a = _twice_gelu(act.astype(BF16)) * g_ref[:, cs]
        part = jnp.dot(a, v_ref[cs, :], preferred_element_type=F32)
        acc = part if acc is None else acc + part
    o_ref[...] += acc


def _experts(n2, u_bf, v_bf, gates, tt, te, sub):
    M, D = n2.shape
    E = u_bf.shape[0]
    return pl.pallas_call(
        functools.partial(_expert_kernel, sub=sub),
        grid=(M // tt, E // te),
        in_specs=[
            pl.BlockSpec((tt, D), lambda i, k: (i, 0)),
            pl.BlockSpec((te, D), lambda i, k: (k, 0)),
            pl.BlockSpec((te, D), lambda i, k: (k, 0)),
            pl.BlockSpec((tt, te), lambda i, k: (i, k)),
        ],
        out_specs=pl.BlockSpec((tt, D), lambda i, k: (i, 0)),
        out_shape=jax.ShapeDtypeStruct((M, D), F32),
        compiler_params=_params(("parallel", "arbitrary")),
    )(n2, u_bf, v_bf, gates)


def _finish_kernel(h_ref, p_ref, g_ref, o_ref, *, final_norm):
    h2 = h_ref[...] + p_ref[...]
    if final_norm:
        ms = jnp.mean(h2 * h2, axis=-1, keepdims=True)
        h2 = h2 * lax.rsqrt(ms + EPS) * g_ref[...]
    o_ref[...] = h2


def _finish(h1, peer_out, g, tm, final_norm):
    M, D = h1.shape
    return pl.pallas_call(
        functools.partial(_finish_kernel, final_norm=final_norm),
        grid=(M // tm,),
        in_specs=[
            pl.BlockSpec((tm, D), lambda i: (i, 0)),
            pl.BlockSpec((tm, D), lambda i: (i, 0)),
            pl.BlockSpec((1, D), lambda i: (0, 0)),
        ],
        out_specs=pl.BlockSpec((tm, D), lambda i: (i, 0)),
        out_shape=jax.ShapeDtypeStruct((M, D), F32),
        compiler_params=_params(("parallel",)),
    )(h1, peer_out, g.reshape(1, D))


def _tile(n, want):
    t = min(n, want)
    assert n % t == 0, (n, t)
    return t


def kernel(x, mix_norm_g, w_in, conv_w, conv_b, lru_w_a, lru_b_a, lru_w_x, lru_b_x, lru_lambda,
           gmlp_ln_g, gmlp_ln_b, gmlp_w_s, gmlp_b_s, w_out, ffn_norm_g, peer_w_q, peer_sub_keys,
           peer_u, peer_v, final_norm_g):
    B, S, D = x.shape
    M = B * S
    depth = w_in.shape[0]
    assert S % CHUNK == 0 and M % ROUTE_T == 0
    h = x.reshape(M, D)
    for l in range(depth):
        z4 = _mix_in(h, mix_norm_g[l], w_in[l].astype(BF16), _tile(M, 512))
        cw = z4.shape[1] // 4
        hf, hb = _lru(z4.reshape(B, S, 4 * cw), conv_w[l], conv_b[l],
                      lru_w_a[l].astype(BF16), lru_b_a[l], lru_w_x[l].astype(BF16), lru_b_x[l],
                      lru_lambda[l], _tile(S, 256))
        bs_full = jnp.repeat(gmlp_b_s[l].T, cw // GMLP_HEADS, axis=1)
        b_out = _gmlp(z4, gmlp_ln_g[l], gmlp_ln_b[l], gmlp_w_s[l].astype(BF16), bs_full, _tile(M, 512))
        h1, n2 = _out_proj(hf.reshape(M, cw), hb.reshape(M, cw), z4, b_out, h,
                           w_out[l].astype(BF16), ffn_norm_g[l], _tile(M, 256))
        keys = peer_sub_keys[l].reshape(PEER_HEADS * 2, PEER_KEYS, -1).astype(BF16)
        gates = _route(n2, peer_w_q[l].astype(BF16), keys)
        peer_out = _experts(n2, peer_u[l].astype(BF16), peer_v[l].astype(BF16), gates,
                            _tile(M, 1024), 512, 256)
        h = _finish(h1, peer_out, final_norm_g, _tile(M, 512), l == depth - 1)
    return h.reshape(B, S, D)
```

```python
import functools

import jax
import jax.numpy as jnp
from jax import lax
from jax.experimental import pallas as pl
from jax.experimental.pallas import tpu as pltpu

F32 = jnp.float32
BF16 = jnp.bfloat16

EPS = 1e-6
LRU_C = 8.0
LRU_HEADS = 8
CONV_WIDTH = 4
CONV_LEFT = 2
GMLP_HEADS = 8
CHUNK = 128
PEER_HEADS = 8
PEER_KEYS = 128
PEER_TOPK = 16

LANES = 128
SUBLANES = 8
VMEM_LIMIT = 56 * 1024 * 1024

_NT = (((1,), (1,)), ((), ()))


def _gelu(x):
    c = 0.7978845608028654
    return 0.5 * x * (1.0 + jnp.tanh(c * (x + 0.044715 * (x * x * x))))


def _twice_gelu(x):
    c = 0.7978845608028654
    return x * (1.0 + jnp.tanh(x * (c + (c * 0.044715) * (x * x))))


def _params(sem):
    return pltpu.CompilerParams(dimension_semantics=sem, vmem_limit_bytes=VMEM_LIMIT)


def _mix_in_kernel(x_ref, g_ref, w_ref, o_ref, n_scr):
    j = pl.program_id(1)

    @pl.when(j == 0)
    def _():
        x = x_ref[...]
        ms = jnp.mean(x * x, axis=-1, keepdims=True)
        n_scr[...] = (x * lax.rsqrt(ms + EPS) * g_ref[...]).astype(BF16)

    acc = jnp.dot(n_scr[...], w_ref[...], preferred_element_type=F32)

    @pl.when(j == 0)
    def _():
        o_ref[...] = acc

    @pl.when(j > 0)
    def _():
        o_ref[...] = _gelu(acc)


def _mix_in(x2, g, w_bf, tm):
    M, D = x2.shape
    N = w_bf.shape[1]
    tn = N // 4
    return pl.pallas_call(
        _mix_in_kernel,
        grid=(M // tm, 4),
        in_specs=[
            pl.BlockSpec((tm, D), lambda i, j: (i, 0)),
            pl.BlockSpec((1, D), lambda i, j: (0, 0)),
            pl.BlockSpec((D, tn), lambda i, j: (0, j)),
        ],
        out_specs=pl.BlockSpec((tm, tn), lambda i, j: (i, j)),
        out_shape=jax.ShapeDtypeStruct((M, N), F32),
        scratch_shapes=[pltpu.VMEM((tm, D), BF16)],
        compiler_params=_params(("parallel", "arbitrary")),
    )(x2, g.reshape(1, D), w_bf)


def _lru_kernel(xf_ref, xfp_ref, xfn_ref, xb_ref, xbp_ref, xbn_ref,
                cw_ref, cb_ref, wa_ref, ba_ref, wx_ref, bx_ref, lam_ref,
                hf_ref, hb_ref,
                a_scr, u_scr, cf_scr, cr_scr, *, tt, nt):
    j = pl.program_id(1)
    C = xf_ref.shape[-1]
    hd = C // LRU_HEADS

    @pl.when(j == 0)
    def _():
        cf_scr[...] = jnp.zeros_like(cf_scr)
        cr_scr[...] = jnp.zeros_like(cr_scr)

    def gates(cur_ref, prev_ref, next_ref, jt, d):
        xe = jnp.concatenate([jnp.where(jt > 0, prev_ref[0], 0.0), cur_ref[0],
                              jnp.where(jt < nt - 1, next_ref[0], 0.0)], axis=0)
        rows = tt + 2 * SUBLANES
        xc = cb_ref[...] + jnp.zeros((tt, C), F32)
        for k in range(CONV_WIDTH):
            shift = (CONV_LEFT - k) % rows
            xs = pltpu.roll(xe, shift, 0) if shift else xe
            xc = xc + xs[SUBLANES:SUBLANES + tt, :] * cw_ref[k:k + 1, :]
        xcb = xc.astype(BF16)
        ra = jnp.concatenate(
            [jnp.dot(xcb[:, h * hd:(h + 1) * hd], wa_ref[d, h], preferred_element_type=F32)
             for h in range(LRU_HEADS)], axis=1) + ba_ref[d:d + 1, :]
        rx = jnp.concatenate(
            [jnp.dot(xcb[:, h * hd:(h + 1) * hd], wx_ref[d, h], preferred_element_type=F32)
             for h in range(LRU_HEADS)], axis=1) + bx_ref[d:d + 1, :]
        ta = jnp.tanh(ra)
        tx = jnp.tanh(rx)
        nl = -lam_ref[d:d + 1, :]
        sp = jnp.maximum(nl, 0.0) + jnp.log(1.0 + jnp.exp(-jnp.abs(nl)))
        c = (-0.5 * LRU_C * 1.4426950408889634) * sp
        a = jnp.exp2(c + c * ta)
        a_scr[...] = a
        y = 1.0 - a * a
        u_scr[...] = jnp.where(y > 0.0, (0.5 * y) * lax.rsqrt(y), 0.0) * ((1.0 + tx) * xc)

    row = lax.broadcasted_iota(jnp.int32, (SUBLANES, C), 0)
    ng = tt // SUBLANES

    gates(xf_ref, xfp_ref, xfn_ref, j, 0)

    def fwd_body(g, carry):
        r0 = pl.multiple_of(g * SUBLANES, SUBLANES)
        a = a_scr[pl.ds(r0, SUBLANES), :]
        u = u_scr[pl.ds(r0, SUBLANES), :]
        for s in (1, 2, 4):
            m = row >= s
            u = u + a * jnp.where(m, pltpu.roll(u, s, 0), 0.0)
            a = a * jnp.where(m, pltpu.roll(a, s, 0), 1.0)
        h = u + a * carry
        hf_ref[0, pl.ds(r0, SUBLANES), :] = h
        return h[SUBLANES - 1:SUBLANES, :]

    cf_scr[...] = lax.fori_loop(0, ng, fwd_body, cf_scr[...], unroll=4)

    gates(xb_ref, xbp_ref, xbn_ref, nt - 1 - j, 1)

    def bwd_body(gi, carry):
        g = ng - 1 - gi
        r0 = pl.multiple_of(g * SUBLANES, SUBLANES)
        a = a_scr[pl.ds(r0, SUBLANES), :]
        u = u_scr[pl.ds(r0, SUBLANES), :]
        for s in (1, 2, 4):
            m = row < SUBLANES - s
            u = u + a * jnp.where(m, pltpu.roll(u, SUBLANES - s, 0), 0.0)
            a = a * jnp.where(m, pltpu.roll(a, SUBLANES - s, 0), 1.0)
        h = u + a * carry
        hb_ref[0, pl.ds(r0, SUBLANES), :] = h
        return h[0:1, :]

    cr_scr[...] = lax.fori_loop(0, ng, bwd_body, cr_scr[...], unroll=4)


def _lru(z3, conv_w, conv_b, w_a_bf, b_a, w_x_bf, b_x, lam, tt):
    B, S, _ = z3.shape
    C = conv_w.shape[-1]
    nt = S // tt
    tb = tt // SUBLANES
    nb8 = S // SUBLANES

    def cur(jt):
        return lambda b, j: (b, jt(j), 0)

    def prev(jt):
        return lambda b, j: (b, jnp.maximum(jt(j) * tb - 1, 0), 0)

    def nxt(jt):
        return lambda b, j: (b, jnp.minimum((jt(j) + 1) * tb, nb8 - 1), 0)

    fw = lambda j: j
    bw = lambda j: nt - 1 - j
    full = lambda shape: pl.BlockSpec(shape, lambda b, j: (0,) * len(shape))
    return pl.pallas_call(
        functools.partial(_lru_kernel, tt=tt, nt=nt),
        grid=(B, nt),
        in_specs=[
            pl.BlockSpec((1, tt, C), cur(fw)),
            pl.BlockSpec((1, SUBLANES, C), prev(fw)),
            pl.BlockSpec((1, SUBLANES, C), nxt(fw)),
            pl.BlockSpec((1, tt, C), cur(bw)),
            pl.BlockSpec((1, SUBLANES, C), prev(bw)),
            pl.BlockSpec((1, SUBLANES, C), nxt(bw)),
            full((CONV_WIDTH, C)), full((1, C)),
            full(w_a_bf.shape), full((2, C)), full(w_x_bf.shape), full((2, C)), full((2, C)),
        ],
        out_specs=[
            pl.BlockSpec((1, tt, C), cur(fw)),
            pl.BlockSpec((1, tt, C), cur(bw)),
        ],
        out_shape=[jax.ShapeDtypeStruct((B, S, C), F32), jax.ShapeDtypeStruct((B, S, C), F32)],
        scratch_shapes=[
            pltpu.VMEM((tt, C), F32),
            pltpu.VMEM((tt, C), F32),
            pltpu.VMEM((1, C), F32),
            pltpu.VMEM((1, C), F32),
        ],
        compiler_params=_params(("parallel", "arbitrary")),
    )(z3, z3, z3, z3, z3, z3, conv_w, conv_b.reshape(1, C), w_a_bf, b_a, w_x_bf, b_x, lam)


def _gmlp_kernel(u_ref, v_ref, lng_ref, lnb_ref, ws_ref, bs_ref, o_ref, *, tc):
    v = v_ref[...]
    mu = jnp.mean(v, axis=-1, keepdims=True)
    vc = v - mu
    var = jnp.mean(vc * vc, axis=-1, keepdims=True)
    vb = (vc * lax.rsqrt(var + EPS) * lng_ref[...] + lnb_ref[...]).astype(BF16)
    hd = v.shape[-1] // GMLP_HEADS
    for c in range(tc // CHUNK):
        rows = slice(c * CHUNK, (c + 1) * CHUNK)
        mixed = jnp.concatenate(
            [jnp.dot(ws_ref[h], vb[rows, h * hd:(h + 1) * hd], preferred_element_type=F32)
             for h in range(GMLP_HEADS)], axis=1)
        o_ref[rows, :] = (u_ref[rows, :] * (mixed + bs_ref[...])).astype(BF16)


def _gmlp(z4, ln_g, ln_b, w_s_bf, bs_full, tc):
    M = z4.shape[0]
    C = ln_g.shape[-1]
    return pl.pallas_call(
        functools.partial(_gmlp_kernel, tc=tc),
        grid=(M // tc,),
        in_specs=[
            pl.BlockSpec((tc, C), lambda i: (i, 2)),
            pl.BlockSpec((tc, C), lambda i: (i, 3)),
            pl.BlockSpec((1, C), lambda i: (0, 0)),
            pl.BlockSpec((1, C), lambda i: (0, 0)),
            pl.BlockSpec(w_s_bf.shape, lambda i: (0, 0, 0)),
            pl.BlockSpec((CHUNK, C), lambda i: (0, 0)),
        ],
        out_specs=pl.BlockSpec((tc, C), lambda i: (i, 0)),
        out_shape=jax.ShapeDtypeStruct((M, C), BF16),
        compiler_params=_params(("parallel",)),
    )(z4, z4, ln_g.reshape(1, C), ln_b.reshape(1, C), w_s_bf, bs_full)


def _out_proj_kernel(hf_ref, hb_ref, gr_ref, bo_ref, x_ref, w_ref, g_ref, h1_ref, n2_ref):
    ca = hf_ref.shape[-1]
    a_out = ((hf_ref[...] + hb_ref[...]) * gr_ref[...]).astype(BF16)
    acc = jnp.dot(a_out, w_ref[0:ca, :], preferred_element_type=F32)
    acc = acc + jnp.dot(bo_ref[...], w_ref[ca:, :], preferred_element_type=F32)
    h1 = x_ref[...] + acc
    h1_ref[...] = h1
    ms = jnp.mean(h1 * h1, axis=-1, keepdims=True)
    n2_ref[...] = (h1 * lax.rsqrt(ms + EPS) * g_ref[...]).astype(BF16)


def _out_proj(hf, hb, z4, b_out, x2, w_bf, g, tm):
    M, D = x2.shape
    ca = hf.shape[-1]
    cb = b_out.shape[-1]
    return pl.pallas_call(
        _out_proj_kernel,
        grid=(M // tm,),
        in_specs=[
            pl.BlockSpec((tm, ca), lambda i: (i, 0)),
            pl.BlockSpec((tm, ca), lambda i: (i, 0)),
            pl.BlockSpec((tm, ca), lambda i: (i, 1)),
            pl.BlockSpec((tm, cb), lambda i: (i, 0)),
            pl.BlockSpec((tm, D), lambda i: (i, 0)),
            pl.BlockSpec(w_bf.shape, lambda i: (0, 0)),
            pl.BlockSpec((1, D), lambda i: (0, 0)),
        ],
        out_specs=[pl.BlockSpec((tm, D), lambda i: (i, 0)), pl.BlockSpec((tm, D), lambda i: (i, 0))],
        out_shape=[jax.ShapeDtypeStruct((M, D), F32), jax.ShapeDtypeStruct((M, D), BF16)],
        compiler_params=_params(("parallel",)),
    )(hf, hb, z4, b_out, x2, w_bf, g.reshape(1, D))


ROUTE_T = LANES
PLANE = ROUTE_T + SUBLANES


def _batcher_pairs(n):
    pairs = []

    def merge(lo, m, r):
        step = r * 2
        if step < m:
            merge(lo, m, step)
            merge(lo + r, m, step)
            pairs.extend((i, i + r) for i in range(lo + r, lo + m - r, step))
        else:
            pairs.append((lo, lo + r))

    def sort(lo, m):
        if m > 1:
            sort(lo, m // 2)
            sort(lo + m // 2, m // 2)
            merge(lo, m, 1)

    sort(0, n)
    return pairs


def _beats(x, y):
    return (x[0] > y[0]) | ((x[0] == y[0]) & (x[1] < y[1]))


def _best(x, y):
    g = _beats(x, y)
    return (jnp.maximum(x[0], y[0]),) + tuple(jnp.where(g, p, q) for p, q in zip(x[1:], y[1:]))


def _compare_exchange(x, y):
    g = _beats(x, y)
    hi = (jnp.maximum(x[0], y[0]),) + tuple(jnp.where(g, p, q) for p, q in zip(x[1:], y[1:]))
    lo = (jnp.minimum(x[0], y[0]),) + tuple(jnp.where(g, q, p) for p, q in zip(x[1:], y[1:]))
    return hi, lo


def _sort_desc(items):
    items = list(items)
    for i, j in _batcher_pairs(len(items)):
        items[i], items[j] = _compare_exchange(items[i], items[j])
    return items


def _bitonic_merge(items):
    items = list(items)
    n = len(items)
    d = n // 2
    while d >= 1:
        for i in range(n):
            if i & d == 0:
                items[i], items[i + d] = _compare_exchange(items[i], items[i + d])
        d //= 2
    return items


def _merge_top(x, y):
    n = len(x)
    return _bitonic_merge([_best(x[k], y[n - 1 - k]) for k in range(n)])


def _merge_full(x, y):
    return _bitonic_merge(list(x) + list(reversed(y)))


def _route_kernel(n_ref, wq_ref, keys_ref, g_ref, q_scr, s_scr, sv_scr, si_scr, w_scr, i_scr, j_scr,
                  wt_scr, it_scr, jt_scr, gs_scr):
    T = ROUTE_T
    K = PEER_TOPK
    NK = PEER_KEYS
    q_scr[...] = jnp.dot(n_ref[...], wq_ref[...], preferred_element_type=F32).astype(BF16)

    H = PEER_HEADS
    row8 = lax.broadcasted_iota(jnp.int32, (SUBLANES, T), 0).astype(F32)

    for u in range(2 * H):
        s_scr[u * NK:(u + 1) * NK, :] = lax.dot_general(
            keys_ref[u], q_scr[:, u * NK:(u + 1) * NK], _NT, preferred_element_type=F32)

    def unit_body(u, _):
        c0 = pl.multiple_of(u * NK, NK)
        items = [(s_scr[pl.ds(c0 + b * SUBLANES, SUBLANES), :], row8 + float(b * SUBLANES))
                 for b in range(NK // SUBLANES)]
        items = _sort_desc(items)
        half = jnp.bitwise_and(u, 1)
        head = jnp.right_shift(u, 1)
        for k, (v, i) in enumerate(items):
            r0 = pl.multiple_of(((half * K + k) * H + head) * SUBLANES, SUBLANES)
            sv_scr[pl.ds(r0, SUBLANES), :] = v
            si_scr[pl.ds(r0, SUBLANES), :] = i
        return 0

    lax.fori_loop(0, 2 * H, unit_body, 0)

    def half_top(half):
        def residue_list(r):
            starts = [(half * K + k) * H * SUBLANES + r for k in range(K)]
            return (tuple(sv_scr[pl.ds(st, H, stride=SUBLANES), :] for st in starts),
                    tuple(si_scr[pl.ds(st, H, stride=SUBLANES), :] for st in starts))

        def merge_body(r, acc):
            new = residue_list(r)
            out = _merge_top(list(zip(*acc)), list(zip(*new)))
            return tuple(x[0] for x in out), tuple(x[1] for x in out)

        vals, idxs = lax.fori_loop(1, SUBLANES, merge_body, residue_list(0))
        return list(zip(vals, idxs))

    a = half_top(0)
    b = half_top(1)

    def pair(p, r):
        code = jnp.full((H, T), float(p * K + r), F32)
        return (a[p][0] + b[r][0], code, a[p][1] * float(NK) + b[r][1])

    rows = [[pair(p, r) for r in range(K // (p + 1))] for p in range(K)]
    singles = [rows[p][0] for p in range(8, K)]
    l1 = _merge_full(rows[1], singles)
    l2 = _sort_desc(rows[2] + rows[3] + rows[4] + rows[5] + rows[6])
    top = _merge_top(rows[0], l1)
    top = _merge_top(top, l2)
    top[K - 2] = _best(top[K - 2], rows[7][1])
    top[K - 1] = _best(top[K - 1], rows[7][0])
    top = _bitonic_merge(top)

    es = [jnp.exp(t[0] - top[0][0]) for t in top]
    z = es[0]
    for e in es[1:]:
        z = z + e
    inv = 0.5 / z
    for k in range(K):
        ident = top[k][2]
        ii = jnp.floor(ident * (1.0 / NK))
        w_scr[k * H:(k + 1) * H, :] = es[k] * inv
        i_scr[k * H:(k + 1) * H, :] = ii
        j_scr[k * H:(k + 1) * H, :] = ident - ii * float(NK)

    wt_scr[...] = w_scr[...].T
    it_scr[...] = i_scr[...].T
    jt_scr[...] = j_scr[...].T

    sub = lax.broadcasted_iota(jnp.int32, (NK, PEER_HEADS * K), 0).astype(F32)

    zero_blk = jnp.zeros((PEER_HEADS * K, NK), BF16)

    def factors(t):
        wrow = wt_scr[pl.ds(t, 1), :]
        irow = it_scr[pl.ds(t, 1), :]
        jrow = jt_scr[pl.ds(t, 1), :]
        p1 = jnp.where(sub == irow, wrow, 0.0).astype(BF16)
        p2 = jnp.where(sub == jrow, 1.0, 0.0).astype(BF16).T
        return p1, p2

    def pair_body(q, _):
        t0 = 2 * q
        p1a, p2a = factors(t0)
        p1b, p2b = factors(t0 + 1)
        lhs = jnp.concatenate([p1a, p1b], axis=1)
        rhs = jnp.concatenate([jnp.concatenate([p2a, zero_blk], axis=1),
                               jnp.concatenate([zero_blk, p2b], axis=1)], axis=0)
        g = jnp.dot(lhs, rhs, preferred_element_type=F32)
        for blk in range(NK // SUBLANES):
            rows = slice(blk * SUBLANES, (blk + 1) * SUBLANES)
            base = blk * SUBLANES * PLANE + t0
            gs_scr[pl.ds(base, SUBLANES, stride=PLANE), :] = g[rows, 0:NK]
            gs_scr[pl.ds(base + 1, SUBLANES, stride=PLANE), :] = g[rows, NK:2 * NK]
        return 0

    lax.fori_loop(0, T // 2, pair_body, 0, unroll=True)

    for i in range(NK):
        g_ref[:, i * NK:(i + 1) * NK] = gs_scr[i * PLANE:i * PLANE + T, :].astype(BF16)


def _route(n2, wq_bf, keys_bf):
    M, D = n2.shape
    Q = wq_bf.shape[1]
    T = ROUTE_T
    NK = PEER_KEYS
    HK = PEER_HEADS * PEER_TOPK
    return pl.pallas_call(
        _route_kernel,
        grid=(M // T,),
        in_specs=[
            pl.BlockSpec((T, D), lambda i: (i, 0)),
            pl.BlockSpec((D, Q), lambda i: (0, 0)),
            pl.BlockSpec(keys_bf.shape, lambda i: (0, 0, 0)),
        ],
        out_specs=pl.BlockSpec((T, NK * NK), lambda i: (i, 0)),
        out_shape=jax.ShapeDtypeStruct((M, NK * NK), BF16),
        scratch_shapes=[
            pltpu.VMEM((T, Q), BF16),
            pltpu.VMEM((2 * PEER_HEADS * NK, T), F32),
            pltpu.VMEM((2 * HK * SUBLANES, T), F32), pltpu.VMEM((2 * HK * SUBLANES, T), F32),
            pltpu.VMEM((HK, T), F32), pltpu.VMEM((HK, T), F32), pltpu.VMEM((HK, T), F32),
            pltpu.VMEM((T, HK), F32), pltpu.VMEM((T, HK), F32), pltpu.VMEM((T, HK), F32),
            pltpu.VMEM((NK * PLANE, NK), F32),
        ],
        compiler_params=_params(("parallel",)),
    )(n2, wq_bf, keys_bf)


def _expert_kernel(n_ref, u_ref, v_ref, g_ref, h_ref, fg_ref, o_ref, *, sub, final_norm):
    k = pl.program_id(1)

    @pl.when(k == 0)
    def _():
        o_ref[...] = jnp.zeros_like(o_ref)

    n = n_ref[...]
    te = u_ref.shape[0]
    acc = None
    for c in range(te // sub):
        cs = slice(c * sub, (c + 1) * sub)
        act = lax.dot_general(n, u_ref[cs, :], _NT, preferred_element_type=F32)
        a = _twice_gelu(act.astype(BF16)) * g_ref[:, cs]
        part = jnp.dot(a, v_ref[cs, :], preferred_element_type=F32)
        acc = part if acc is None else acc + part
    o_ref[...] += acc
    hs = h_ref.shape[0]
    r0 = pl.multiple_of(k * hs, hs)
    o_ref[pl.ds(r0, hs), :] += h_ref[...]

    if final_norm:
        @pl.when(k == pl.num_programs(1) - 1)
        def _():
            h2 = o_ref[...]
            ms = jnp.mean(h2 * h2, axis=-1, keepdims=True)
            o_ref[...] = h2 * lax.rsqrt(ms + EPS) * fg_ref[...]


def _experts(n2, u_bf, v_bf, gates, h1, fg, tt, te, sub, final_norm):
    M, D = n2.shape
    nk = u_bf.shape[0] // te
    hs = tt // nk
    assert hs * nk == tt and hs % SUBLANES == 0
    return pl.pallas_call(
        functools.partial(_expert_kernel, sub=sub, final_norm=final_norm),
        grid=(M // tt, nk),
        in_specs=[
            pl.BlockSpec((tt, D), lambda i, k: (i, 0)),
            pl.BlockSpec((te, D), lambda i, k: (k, 0)),
            pl.BlockSpec((te, D), lambda i, k: (k, 0)),
            pl.BlockSpec((tt, te), lambda i, k: (i, k)),
            pl.BlockSpec((hs, D), lambda i, k: (i * nk + k, 0)),
            pl.BlockSpec((1, D), lambda i, k: (0, 0)),
        ],
        out_specs=pl.BlockSpec((tt, D), lambda i, k: (i, 0)),
        out_shape=jax.ShapeDtypeStruct((M, D), F32),
        compiler_params=_params(("parallel", "arbitrary")),
    )(n2, u_bf, v_bf, gates, h1, fg.reshape(1, D))


def _tile(n, want):
    t = min(n, want)
    assert n % t == 0, (n, t)
    return t


def kernel(x, mix_norm_g, w_in, conv_w, conv_b, lru_w_a, lru_b_a, lru_w_x, lru_b_x, lru_lambda,
           gmlp_ln_g, gmlp_ln_b, gmlp_w_s, gmlp_b_s, w_out, ffn_norm_g, peer_w_q, peer_sub_keys,
           peer_u, peer_v, final_norm_g):
    B, S, D = x.shape
    M = B * S
    depth = w_in.shape[0]
    assert S % CHUNK == 0 and M % ROUTE_T == 0
    h = x.reshape(M, D)
    for l in range(depth):
        z4 = _mix_in(h, mix_norm_g[l], w_in[l].astype(BF16), _tile(M, 512))
        cw = z4.shape[1] // 4
        hf, hb = _lru(z4.reshape(B, S, 4 * cw), conv_w[l], conv_b[l],
                      (0.5 * lru_w_a[l]).astype(BF16), 0.5 * lru_b_a[l],
                      (0.5 * lru_w_x[l]).astype(BF16), 0.5 * lru_b_x[l],
                      lru_lambda[l], _tile(S, 256))
        bs_full = jnp.repeat(gmlp_b_s[l].T, cw // GMLP_HEADS, axis=1)
        b_out = _gmlp(z4, gmlp_ln_g[l], gmlp_ln_b[l], gmlp_w_s[l].astype(BF16), bs_full, _tile(M, 512))
        h1, n2 = _out_proj(hf.reshape(M, cw), hb.reshape(M, cw), z4, b_out, h,
                           w_out[l].astype(BF16), ffn_norm_g[l], _tile(M, 256))
        keys = peer_sub_keys[l].reshape(PEER_HEADS * 2, PEER_KEYS, -1).astype(BF16)
        gates = _route(n2, peer_w_q[l].astype(BF16), keys)
        h = _experts(n2, peer_u[l].astype(BF16), peer_v[l].astype(BF16), gates, h1, final_norm_g,
                     _tile(M, 1024), 512, 256, l == depth - 1)
    return h.reshape(B, S, D)
```

```python
import functools

import jax
import jax.numpy as jnp
from jax import lax
from jax.experimental import pallas as pl
from jax.experimental.pallas import tpu as pltpu

F32 = jnp.float32
BF16 = jnp.bfloat16

EPS = 1e-6
LRU_C = 8.0
LRU_HEADS = 8
CONV_WIDTH = 4
CONV_LEFT = 2
GMLP_HEADS = 8
CHUNK = 128
PEER_HEADS = 8
PEER_KEYS = 128
PEER_TOPK = 16

LANES = 128
SUBLANES = 8
VMEM_LIMIT = 56 * 1024 * 1024

_NT = (((1,), (1,)), ((), ()))


def _gelu(x):
    c = 0.7978845608028654
    return 0.5 * x * (1.0 + jnp.tanh(c * (x + 0.044715 * (x * x * x))))


def _twice_gelu(x):
    c = 0.7978845608028654
    return x * (1.0 + jnp.tanh(x * (c + (c * 0.044715) * (x * x))))


EXPERTS_VMEM_LIMIT = 60000 * 1024


def _params(sem, vmem_limit=VMEM_LIMIT):
    return pltpu.CompilerParams(dimension_semantics=sem, vmem_limit_bytes=vmem_limit)


def _mix_in_kernel(x_ref, g_ref, w_ref, o_ref, n_scr):
    j = pl.program_id(1)

    @pl.when(j == 0)
    def _():
        x = x_ref[...]
        ms = jnp.mean(x * x, axis=-1, keepdims=True)
        n_scr[...] = (x * lax.rsqrt(ms + EPS) * g_ref[...]).astype(BF16)

    acc = jnp.dot(n_scr[...], w_ref[...], preferred_element_type=F32)

    @pl.when(j == 0)
    def _():
        o_ref[...] = acc

    @pl.when(j > 0)
    def _():
        o_ref[...] = _gelu(acc)


def _mix_in(x2, g, w_bf, tm):
    M, D = x2.shape
    N = w_bf.shape[1]
    tn = N // 4
    return pl.pallas_call(
        _mix_in_kernel,
        grid=(M // tm, 4),
        in_specs=[
            pl.BlockSpec((tm, D), lambda i, j: (i, 0)),
            pl.BlockSpec((1, D), lambda i, j: (0, 0)),
            pl.BlockSpec((D, tn), lambda i, j: (0, j)),
        ],
        out_specs=pl.BlockSpec((tm, tn), lambda i, j: (i, j)),
        out_shape=jax.ShapeDtypeStruct((M, N), F32),
        scratch_shapes=[pltpu.VMEM((tm, D), BF16)],
        compiler_params=_params(("parallel", "arbitrary")),
    )(x2, g.reshape(1, D), w_bf)


def _lru_kernel(xf_ref, xfp_ref, xfn_ref, xb_ref, xbp_ref, xbn_ref,
                cw_ref, cb_ref, wa_ref, ba_ref, wx_ref, bx_ref, lam_ref,
                hf_ref, hb_ref,
                a_scr, u_scr, cf_scr, cr_scr, *, tt, nt):
    j = pl.program_id(1)
    C = xf_ref.shape[-1]
    hd = C // LRU_HEADS

    @pl.when(j == 0)
    def _():
        cf_scr[...] = jnp.zeros_like(cf_scr)
        cr_scr[...] = jnp.zeros_like(cr_scr)

    def gates(cur_ref, prev_ref, next_ref, jt, d):
        xe = jnp.concatenate([jnp.where(jt > 0, prev_ref[0], 0.0), cur_ref[0],
                              jnp.where(jt < nt - 1, next_ref[0], 0.0)], axis=0)
        rows = tt + 2 * SUBLANES
        xc = cb_ref[...] + jnp.zeros((tt, C), F32)
        for k in range(CONV_WIDTH):
            shift = (CONV_LEFT - k) % rows
            xs = pltpu.roll(xe, shift, 0) if shift else xe
            xc = xc + xs[SUBLANES:SUBLANES + tt, :] * cw_ref[k:k + 1, :]
        xcb = xc.astype(BF16)
        ra = jnp.concatenate(
            [jnp.dot(xcb[:, h * hd:(h + 1) * hd], wa_ref[d, h], preferred_element_type=F32)
             for h in range(LRU_HEADS)], axis=1) + ba_ref[d:d + 1, :]
        rx = jnp.concatenate(
            [jnp.dot(xcb[:, h * hd:(h + 1) * hd], wx_ref[d, h], preferred_element_type=F32)
             for h in range(LRU_HEADS)], axis=1) + bx_ref[d:d + 1, :]
        ta = jnp.tanh(ra)
        tx = jnp.tanh(rx)
        nl = -lam_ref[d:d + 1, :]
        sp = jnp.maximum(nl, 0.0) + jnp.log(1.0 + jnp.exp(-jnp.abs(nl)))
        c = (-0.5 * LRU_C * 1.4426950408889634) * sp
        a = jnp.exp2(c + c * ta)
        a_scr[...] = a
        y = 1.0 - a * a
        u_scr[...] = jnp.where(y > 0.0, (0.5 * y) * lax.rsqrt(y), 0.0) * ((1.0 + tx) * xc)

    row = lax.broadcasted_iota(jnp.int32, (SUBLANES, C), 0)
    ng = tt // SUBLANES

    gates(xf_ref, xfp_ref, xfn_ref, j, 0)

    def fwd_body(g, carry):
        r0 = pl.multiple_of(g * SUBLANES, SUBLANES)
        a = a_scr[pl.ds(r0, SUBLANES), :]
        u = u_scr[pl.ds(r0, SUBLANES), :]
        for s in (1, 2, 4):
            m = row >= s
            u = u + a * jnp.where(m, pltpu.roll(u, s, 0), 0.0)
            a = a * jnp.where(m, pltpu.roll(a, s, 0), 1.0)
        h = u + a * carry
        hf_ref[0, pl.ds(r0, SUBLANES), :] = h
        return h[SUBLANES - 1:SUBLANES, :]

    cf_scr[...] = lax.fori_loop(0, ng, fwd_body, cf_scr[...], unroll=4)

    gates(xb_ref, xbp_ref, xbn_ref, nt - 1 - j, 1)

    def bwd_body(gi, carry):
        g = ng - 1 - gi
        r0 = pl.multiple_of(g * SUBLANES, SUBLANES)
        a = a_scr[pl.ds(r0, SUBLANES), :]
        u = u_scr[pl.ds(r0, SUBLANES), :]
        for s in (1, 2, 4):
            m = row < SUBLANES - s
            u = u + a * jnp.where(m, pltpu.roll(u, SUBLANES - s, 0), 0.0)
            a = a * jnp.where(m, pltpu.roll(a, SUBLANES - s, 0), 1.0)
        h = u + a * carry
        hb_ref[0, pl.ds(r0, SUBLANES), :] = h
        return h[0:1, :]

    cr_scr[...] = lax.fori_loop(0, ng, bwd_body, cr_scr[...], unroll=4)


def _lru(z3, conv_w, conv_b, w_a_bf, b_a, w_x_bf, b_x, lam, tt):
    B, S, _ = z3.shape
    C = conv_w.shape[-1]
    nt = S // tt
    tb = tt // SUBLANES
    nb8 = S // SUBLANES

    def cur(jt):
        return lambda b, j: (b, jt(j), 0)

    def prev(jt):
        return lambda b, j: (b, jnp.maximum(jt(j) * tb - 1, 0), 0)

    def nxt(jt):
        return lambda b, j: (b, jnp.minimum((jt(j) + 1) * tb, nb8 - 1), 0)

    fw = lambda j: j
    bw = lambda j: nt - 1 - j
    full = lambda shape: pl.BlockSpec(shape, lambda b, j: (0,) * len(shape))
    return pl.pallas_call(
        functools.partial(_lru_kernel, tt=tt, nt=nt),
        grid=(B, nt),
        in_specs=[
            pl.BlockSpec((1, tt, C), cur(fw)),
            pl.BlockSpec((1, SUBLANES, C), prev(fw)),
            pl.BlockSpec((1, SUBLANES, C), nxt(fw)),
            pl.BlockSpec((1, tt, C), cur(bw)),
            pl.BlockSpec((1, SUBLANES, C), prev(bw)),
            pl.BlockSpec((1, SUBLANES, C), nxt(bw)),
            full((CONV_WIDTH, C)), full((1, C)),
            full(w_a_bf.shape), full((2, C)), full(w_x_bf.shape), full((2, C)), full((2, C)),
        ],
        out_specs=[
            pl.BlockSpec((1, tt, C), cur(fw)),
            pl.BlockSpec((1, tt, C), cur(bw)),
        ],
        out_shape=[jax.ShapeDtypeStruct((B, S, C), F32), jax.ShapeDtypeStruct((B, S, C), F32)],
        scratch_shapes=[
            pltpu.VMEM((tt, C), F32),
            pltpu.VMEM((tt, C), F32),
            pltpu.VMEM((1, C), F32),
            pltpu.VMEM((1, C), F32),
        ],
        compiler_params=_params(("parallel", "arbitrary")),
    )(z3, z3, z3, z3, z3, z3, conv_w, conv_b.reshape(1, C), w_a_bf, b_a, w_x_bf, b_x, lam)


def _gmlp_kernel(u_ref, v_ref, lng_ref, lnb_ref, ws_ref, bs_ref, o_ref, *, tc):
    v = v_ref[...]
    mu = jnp.mean(v, axis=-1, keepdims=True)
    vc = v - mu
    var = jnp.mean(vc * vc, axis=-1, keepdims=True)
    vb = (vc * lax.rsqrt(var + EPS) * lng_ref[...] + lnb_ref[...]).astype(BF16)
    hd = v.shape[-1] // GMLP_HEADS
    for c in range(tc // CHUNK):
        rows = slice(c * CHUNK, (c + 1) * CHUNK)
        mixed = jnp.concatenate(
            [jnp.dot(ws_ref[h], vb[rows, h * hd:(h + 1) * hd], preferred_element_type=F32)
             for h in range(GMLP_HEADS)], axis=1)
        o_ref[rows, :] = (u_ref[rows, :] * (mixed + bs_ref[...])).astype(BF16)


def _gmlp(z4, ln_g, ln_b, w_s_bf, bs_full, tc):
    M = z4.shape[0]
    C = ln_g.shape[-1]
    return pl.pallas_call(
        functools.partial(_gmlp_kernel, tc=tc),
        grid=(M // tc,),
        in_specs=[
            pl.BlockSpec((tc, C), lambda i: (i, 2)),
            pl.BlockSpec((tc, C), lambda i: (i, 3)),
            pl.BlockSpec((1, C), lambda i: (0, 0)),
            pl.BlockSpec((1, C), lambda i: (0, 0)),
            pl.BlockSpec(w_s_bf.shape, lambda i: (0, 0, 0)),
            pl.BlockSpec((CHUNK, C), lambda i: (0, 0)),
        ],
        out_specs=pl.BlockSpec((tc, C), lambda i: (i, 0)),
        out_shape=jax.ShapeDtypeStruct((M, C), BF16),
        compiler_params=_params(("parallel",)),
    )(z4, z4, ln_g.reshape(1, C), ln_b.reshape(1, C), w_s_bf, bs_full)


def _out_proj_kernel(hf_ref, hb_ref, gr_ref, bo_ref, x_ref, w_ref, g_ref, h1_ref, n2_ref):
    ca = hf_ref.shape[-1]
    a_out = ((hf_ref[...] + hb_ref[...]) * gr_ref[...]).astype(BF16)
    acc = jnp.dot(a_out, w_ref[0:ca, :], preferred_element_type=F32)
    acc = acc + jnp.dot(bo_ref[...], w_ref[ca:, :], preferred_element_type=F32)
    h1 = x_ref[...] + acc
    h1_ref[...] = h1
    ms = jnp.mean(h1 * h1, axis=-1, keepdims=True)
    n2_ref[...] = (h1 * lax.rsqrt(ms + EPS) * g_ref[...]).astype(BF16)


def _out_proj(hf, hb, z4, b_out, x2, w_bf, g, tm):
    M, D = x2.shape
    ca = hf.shape[-1]
    cb = b_out.shape[-1]
    return pl.pallas_call(
        _out_proj_kernel,
        grid=(M // tm,),
        in_specs=[
            pl.BlockSpec((tm, ca), lambda i: (i, 0)),
            pl.BlockSpec((tm, ca), lambda i: (i, 0)),
            pl.BlockSpec((tm, ca), lambda i: (i, 1)),
            pl.BlockSpec((tm, cb), lambda i: (i, 0)),
            pl.BlockSpec((tm, D), lambda i: (i, 0)),
            pl.BlockSpec(w_bf.shape, lambda i: (0, 0)),
            pl.BlockSpec((1, D), lambda i: (0, 0)),
        ],
        out_specs=[pl.BlockSpec((tm, D), lambda i: (i, 0)), pl.BlockSpec((tm, D), lambda i: (i, 0))],
        out_shape=[jax.ShapeDtypeStruct((M, D), F32), jax.ShapeDtypeStruct((M, D), BF16)],
        compiler_params=_params(("parallel",)),
    )(hf, hb, z4, b_out, x2, w_bf, g.reshape(1, D))


ROUTE_T = LANES
PLANE = ROUTE_T + SUBLANES


def _batcher_pairs(n):
    pairs = []

    def merge(lo, m, r):
        step = r * 2
        if step < m:
            merge(lo, m, step)
            merge(lo + r, m, step)
            pairs.extend((i, i + r) for i in range(lo + r, lo + m - r, step))
        else:
            pairs.append((lo, lo + r))

    def sort(lo, m):
        if m > 1:
            sort(lo, m // 2)
            sort(lo + m // 2, m // 2)
            merge(lo, m, 1)

    sort(0, n)
    return pairs


def _beats(x, y):
    return (x[0] > y[0]) | ((x[0] == y[0]) & (x[1] < y[1]))


def _best(x, y):
    g = _beats(x, y)
    return (jnp.maximum(x[0], y[0]),) + tuple(jnp.where(g, p, q) for p, q in zip(x[1:], y[1:]))


def _compare_exchange(x, y):
    g = _beats(x, y)
    hi = (jnp.maximum(x[0], y[0]),) + tuple(jnp.where(g, p, q) for p, q in zip(x[1:], y[1:]))
    lo = (jnp.minimum(x[0], y[0]),) + tuple(jnp.where(g, q, p) for p, q in zip(x[1:], y[1:]))
    return hi, lo


def _sort_desc(items):
    items = list(items)
    for i, j in _batcher_pairs(len(items)):
        items[i], items[j] = _compare_exchange(items[i], items[j])
    return items


def _bitonic_merge(items):
    items = list(items)
    n = len(items)
    d = n // 2
    while d >= 1:
        for i in range(n):
            if i & d == 0:
                items[i], items[i + d] = _compare_exchange(items[i], items[i + d])
        d //= 2
    return items


def _merge_top(x, y):
    n = len(x)
    return _bitonic_merge([_best(x[k], y[n - 1 - k]) for k in range(n)])


def _merge_full(x, y):
    return _bitonic_merge(list(x) + list(reversed(y)))


def _route_kernel(n_ref, wq_ref, keys_ref, g_ref, q_scr, s_scr, sv_scr, si_scr, w_scr, i_scr, j_scr,
                  wt_scr, it_scr, jt_scr, gs_scr):
    T = ROUTE_T
    K = PEER_TOPK
    NK = PEER_KEYS
    q_scr[...] = jnp.dot(n_ref[...], wq_ref[...], preferred_element_type=F32).astype(BF16)

    H = PEER_HEADS
    row8 = lax.broadcasted_iota(jnp.int32, (SUBLANES, T), 0).astype(F32)

    for u in range(2 * H):
        s_scr[u * NK:(u + 1) * NK, :] = lax.dot_general(
            keys_ref[u], q_scr[:, u * NK:(u + 1) * NK], _NT, preferred_element_type=F32)

    def unit_body(u, _):
        c0 = pl.multiple_of(u * NK, NK)
        items = [(s_scr[pl.ds(c0 + b * SUBLANES, SUBLANES), :], row8 + float(b * SUBLANES))
                 for b in range(NK // SUBLANES)]
        items = _sort_desc(items)
        half = jnp.bitwise_and(u, 1)
        head = jnp.right_shift(u, 1)
        for k, (v, i) in enumerate(items):
            r0 = pl.multiple_of(((half * K + k) * H + head) * SUBLANES, SUBLANES)
            sv_scr[pl.ds(r0, SUBLANES), :] = v
            si_scr[pl.ds(r0, SUBLANES), :] = i
        return 0

    lax.fori_loop(0, 2 * H, unit_body, 0)

    def half_top(half):
        def residue_list(r):
            starts = [(half * K + k) * H * SUBLANES + r for k in range(K)]
            return (tuple(sv_scr[pl.ds(st, H, stride=SUBLANES), :] for st in starts),
                    tuple(si_scr[pl.ds(st, H, stride=SUBLANES), :] for st in starts))

        def merge_body(r, acc):
            new = residue_list(r)
            out = _merge_top(list(zip(*acc)), list(zip(*new)))
            return tuple(x[0] for x in out), tuple(x[1] for x in out)

        vals, idxs = lax.fori_loop(1, SUBLANES, merge_body, residue_list(0))
        return list(zip(vals, idxs))

    a = half_top(0)
    b = half_top(1)

    def pair(p, r):
        code = jnp.full((H, T), float(p * K + r), F32)
        return (a[p][0] + b[r][0], code, a[p][1] * float(NK) + b[r][1])

    rows = [[pair(p, r) for r in range(K // (p + 1))] for p in range(K)]
    singles = [rows[p][0] for p in range(8, K)]
    l1 = _merge_full(rows[1], singles)
    l2 = _sort_desc(rows[2] + rows[3] + rows[4] + rows[5] + rows[6])
    top = _merge_top(rows[0], l1)
    top = _merge_top(top, l2)
    top[K - 2] = _best(top[K - 2], rows[7][1])
    top[K - 1] = _best(top[K - 1], rows[7][0])
    top = _bitonic_merge(top)

    es = [jnp.exp(t[0] - top[0][0]) for t in top]
    z = es[0]
    for e in es[1:]:
        z = z + e
    inv = 0.5 / z
    for k in range(K):
        ident = top[k][2]
        ii = jnp.floor(ident * (1.0 / NK))
        w_scr[k * H:(k + 1) * H, :] = es[k] * inv
        i_scr[k * H:(k + 1) * H, :] = ii
        j_scr[k * H:(k + 1) * H, :] = ident - ii * float(NK)

    wt_scr[...] = w_scr[...].T
    it_scr[...] = i_scr[...].T
    jt_scr[...] = j_scr[...].T

    sub = lax.broadcasted_iota(jnp.int32, (NK, PEER_HEADS * K), 0).astype(F32)

    zero_blk = jnp.zeros((PEER_HEADS * K, NK), BF16)

    def factors(t):
        wrow = wt_scr[pl.ds(t, 1), :]
        irow = it_scr[pl.ds(t, 1), :]
        jrow = jt_scr[pl.ds(t, 1), :]
        p1 = jnp.where(sub == irow, wrow, 0.0).astype(BF16)
        p2 = jnp.where(sub == jrow, 1.0, 0.0).astype(BF16).T
        return p1, p2

    def pair_body(q, _):
        t0 = 2 * q
        p1a, p2a = factors(t0)
        p1b, p2b = factors(t0 + 1)
        lhs = jnp.concatenate([p1a, p1b], axis=1)
        rhs = jnp.concatenate([jnp.concatenate([p2a, zero_blk], axis=1),
                               jnp.concatenate([zero_blk, p2b], axis=1)], axis=0)
        g = jnp.dot(lhs, rhs, preferred_element_type=F32)
        for blk in range(NK // SUBLANES):
            rows = slice(blk * SUBLANES, (blk + 1) * SUBLANES)
            base = blk * SUBLANES * PLANE + t0
            gs_scr[pl.ds(base, SUBLANES, stride=PLANE), :] = g[rows, 0:NK]
            gs_scr[pl.ds(base + 1, SUBLANES, stride=PLANE), :] = g[rows, NK:2 * NK]
        return 0

    lax.fori_loop(0, T // 2, pair_body, 0, unroll=True)

    for i in range(NK):
        g_ref[:, i * NK:(i + 1) * NK] = gs_scr[i * PLANE:i * PLANE + T, :].astype(BF16)


def _route(n2, wq_bf, keys_bf):
    M, D = n2.shape
    Q = wq_bf.shape[1]
    T = ROUTE_T
    NK = PEER_KEYS
    HK = PEER_HEADS * PEER_TOPK
    return pl.pallas_call(
        _route_kernel,
        grid=(M // T,),
        in_specs=[
            pl.BlockSpec((T, D), lambda i: (i, 0)),
            pl.BlockSpec((D, Q), lambda i: (0, 0)),
            pl.BlockSpec(keys_bf.shape, lambda i: (0, 0, 0)),
        ],
        out_specs=pl.BlockSpec((T, NK * NK), lambda i: (i, 0)),
        out_shape=jax.ShapeDtypeStruct((M, NK * NK), BF16),
        scratch_shapes=[
            pltpu.VMEM((T, Q), BF16),
            pltpu.VMEM((2 * PEER_HEADS * NK, T), F32),
            pltpu.VMEM((2 * HK * SUBLANES, T), F32), pltpu.VMEM((2 * HK * SUBLANES, T), F32),
            pltpu.VMEM((HK, T), F32), pltpu.VMEM((HK, T), F32), pltpu.VMEM((HK, T), F32),
            pltpu.VMEM((T, HK), F32), pltpu.VMEM((T, HK), F32), pltpu.VMEM((T, HK), F32),
            pltpu.VMEM((NK * PLANE, NK), F32),
        ],
        compiler_params=_params(("parallel",)),
    )(n2, wq_bf, keys_bf)


def _expert_kernel(n_ref, u_ref, v_ref, g_ref, h_ref, fg_ref, o_ref, *, sub, final_norm):
    k = pl.program_id(1)

    @pl.when(k == 0)
    def _():
        o_ref[...] = jnp.zeros_like(o_ref)

    n = n_ref[...]
    te = u_ref.shape[0]
    acc = None
    for c in range(te // sub):
        cs = slice(c * sub, (c + 1) * sub)
        act = lax.dot_general(n, u_ref[cs, :], _NT, preferred_element_type=F32)
        a = _twice_gelu(act.astype(BF16)) * g_ref[:, cs]
        part = jnp.dot(a, v_ref[cs, :], preferred_element_type=F32)
        acc = part if acc is None else acc + part
    o_ref[...] += acc
    hs = h_ref.shape[0]
    r0 = pl.multiple_of(k * hs, hs)
    o_ref[pl.ds(r0, hs), :] += h_ref[...]

    if final_norm:
        @pl.when(k == pl.num_programs(1) - 1)
        def _():
            h2 = o_ref[...]
            ms = jnp.mean(h2 * h2, axis=-1, keepdims=True)
            o_ref[...] = h2 * lax.rsqrt(ms + EPS) * fg_ref[...]


def _experts(n2, u_bf, v_bf, gates, h1, fg, tt, te, sub, final_norm):
    M, D = n2.shape
    nk = u_bf.shape[0] // te
    hs = tt // nk
    assert hs * nk == tt and hs % SUBLANES == 0
    return pl.pallas_call(
        functools.partial(_expert_kernel, sub=sub, final_norm=final_norm),
        grid=(M // tt, nk),
        in_specs=[
            pl.BlockSpec((tt, D), lambda i, k: (i, 0)),
            pl.BlockSpec((te, D), lambda i, k: (k, 0)),
            pl.BlockSpec((te, D), lambda i, k: (k, 0)),
            pl.BlockSpec((tt, te), lambda i, k: (i, k)),
            pl.BlockSpec((hs, D), lambda i, k: (i * nk + k, 0)),
            pl.BlockSpec((1, D), lambda i, k: (0, 0)),
        ],
        out_specs=pl.BlockSpec((tt, D), lambda i, k: (i, 0)),
        out_shape=jax.ShapeDtypeStruct((M, D), F32),
        compiler_params=_params(("parallel", "arbitrary"), EXPERTS_VMEM_LIMIT),
    )(n2, u_bf, v_bf, gates, h1, fg.reshape(1, D))


def _tile(n, want):
    t = min(n, want)
    assert n % t == 0, (n, t)
    return t


def kernel(x, mix_norm_g, w_in, conv_w, conv_b, lru_w_a, lru_b_a, lru_w_x, lru_b_x, lru_lambda,
           gmlp_ln_g, gmlp_ln_b, gmlp_w_s, gmlp_b_s, w_out, ffn_norm_g, peer_w_q, peer_sub_keys,
           peer_u, peer_v, final_norm_g):
    B, S, D = x.shape
    M = B * S
    depth = w_in.shape[0]
    assert S % CHUNK == 0 and M % ROUTE_T == 0
    h = x.reshape(M, D)
    for l in range(depth):
        z4 = _mix_in(h, mix_norm_g[l], w_in[l].astype(BF16), _tile(M, 512))
        cw = z4.shape[1] // 4
        hf, hb = _lru(z4.reshape(B, S, 4 * cw), conv_w[l], conv_b[l],
                      (0.5 * lru_w_a[l]).astype(BF16), 0.5 * lru_b_a[l],
                      (0.5 * lru_w_x[l]).astype(BF16), 0.5 * lru_b_x[l],
                      lru_lambda[l], _tile(S, 256))
        bs_full = jnp.repeat(gmlp_b_s[l].T, cw // GMLP_HEADS, axis=1)
        b_out = _gmlp(z4, gmlp_ln_g[l], gmlp_ln_b[l], gmlp_w_s[l].astype(BF16), bs_full, _tile(M, 512))
        h1, n2 = _out_proj(hf.reshape(M, cw), hb.reshape(M, cw), z4, b_out, h,
                           w_out[l].astype(BF16), ffn_norm_g[l], _tile(M, 256))
        keys = peer_sub_keys[l].reshape(PEER_HEADS * 2, PEER_KEYS, -1).astype(BF16)
        gates = _route(n2, peer_w_q[l].astype(BF16), keys)
        h = _experts(n2, peer_u[l].astype(BF16), peer_v[l].astype(BF16), gates, h1, final_norm_g,
                     _tile(M, 1024), 1024, 512, l == depth - 1)
    return h.reshape(B, S, D)
```

```python
import functools

import jax
import jax.numpy as jnp
from jax import lax
from jax.experimental import pallas as pl
from jax.experimental.pallas import tpu as pltpu

F32 = jnp.float32
BF16 = jnp.bfloat16

EPS = 1e-6
LRU_C = 8.0
LRU_HEADS = 8
CONV_WIDTH = 4
CONV_LEFT = 2
GMLP_HEADS = 8
CHUNK = 128
PEER_HEADS = 8
PEER_KEYS = 128
PEER_TOPK = 16

LANES = 128
SUBLANES = 8
VMEM_LIMIT = 56 * 1024 * 1024

_NT = (((1,), (1,)), ((), ()))


def _gelu(x):
    c = 0.7978845608028654
    return x * (0.5 + 0.5 * jnp.tanh(x * (c + (c * 0.044715) * (x * x))))


def _twice_gelu(x):
    c = 0.7978845608028654
    return x * (1.0 + jnp.tanh(x * (c + (c * 0.044715) * (x * x))))


EXPERTS_VMEM_LIMIT = 60000 * 1024


def _params(sem, vmem_limit=VMEM_LIMIT):
    return pltpu.CompilerParams(dimension_semantics=sem, vmem_limit_bytes=vmem_limit)


def _mix_in_kernel(x_ref, g_ref, w_ref, o_ref, n_scr):
    j = pl.program_id(1)

    @pl.when(j == 0)
    def _():
        x = x_ref[...]
        ms = jnp.mean(x * x, axis=-1, keepdims=True)
        n_scr[...] = (x * lax.rsqrt(ms + EPS) * g_ref[...]).astype(BF16)

    acc = jnp.dot(n_scr[...], w_ref[...], preferred_element_type=F32)

    @pl.when(j == 0)
    def _():
        o_ref[...] = acc

    @pl.when(j > 0)
    def _():
        o_ref[...] = _gelu(acc)


def _mix_in(x2, g, w_bf, tm):
    M, D = x2.shape
    N = w_bf.shape[1]
    tn = N // 4
    return pl.pallas_call(
        _mix_in_kernel,
        grid=(M // tm, 4),
        in_specs=[
            pl.BlockSpec((tm, D), lambda i, j: (i, 0)),
            pl.BlockSpec((1, D), lambda i, j: (0, 0)),
            pl.BlockSpec((D, tn), lambda i, j: (0, j)),
        ],
        out_specs=pl.BlockSpec((tm, tn), lambda i, j: (i, j)),
        out_shape=jax.ShapeDtypeStruct((M, N), F32),
        scratch_shapes=[pltpu.VMEM((tm, D), BF16)],
        compiler_params=_params(("parallel", "arbitrary")),
    )(x2, g.reshape(1, D), w_bf)


def _lru_kernel(xf_ref, xfp_ref, xfn_ref, xb_ref, xbp_ref, xbn_ref,
                cw_ref, cb_ref, wa_ref, ba_ref, wx_ref, bx_ref, lam_ref,
                hf_ref, hb_ref,
                a_scr, u_scr, cf_scr, cr_scr, *, tt, nt):
    j = pl.program_id(1)
    C = xf_ref.shape[-1]
    hd = C // LRU_HEADS

    @pl.when(j == 0)
    def _():
        cf_scr[...] = jnp.zeros_like(cf_scr)
        cr_scr[...] = jnp.zeros_like(cr_scr)

    def gates(cur_ref, prev_ref, next_ref, jt, d):
        xe = jnp.concatenate([jnp.where(jt > 0, prev_ref[0], 0.0), cur_ref[0],
                              jnp.where(jt < nt - 1, next_ref[0], 0.0)], axis=0)
        rows = tt + 2 * SUBLANES
        xc = cb_ref[...] + jnp.zeros((tt, C), F32)
        for k in range(CONV_WIDTH):
            shift = (CONV_LEFT - k) % rows
            xs = pltpu.roll(xe, shift, 0) if shift else xe
            xc = xc + xs[SUBLANES:SUBLANES + tt, :] * cw_ref[k:k + 1, :]
        xcb = xc.astype(BF16)
        ra = jnp.concatenate(
            [jnp.dot(xcb[:, h * hd:(h + 1) * hd], wa_ref[d, h], preferred_element_type=F32)
             for h in range(LRU_HEADS)], axis=1) + ba_ref[d:d + 1, :]
        rx = jnp.concatenate(
            [jnp.dot(xcb[:, h * hd:(h + 1) * hd], wx_ref[d, h], preferred_element_type=F32)
             for h in range(LRU_HEADS)], axis=1) + bx_ref[d:d + 1, :]
        ta = jnp.tanh(ra)
        tx = jnp.tanh(rx)
        nl = -lam_ref[d:d + 1, :]
        sp = jnp.maximum(nl, 0.0) + jnp.log(1.0 + jnp.exp(-jnp.abs(nl)))
        c = (-0.5 * LRU_C * 1.4426950408889634) * sp
        a = jnp.exp2(c + c * ta)
        a_scr[...] = a
        y = 1.0 - a * a
        u_scr[...] = jnp.where(y > 0.0, (0.5 * y) * lax.rsqrt(y), 0.0) * ((1.0 + tx) * xc)

    row = lax.broadcasted_iota(jnp.int32, (SUBLANES, C), 0)
    ng = tt // SUBLANES

    gates(xf_ref, xfp_ref, xfn_ref, j, 0)

    def fwd_body(g, carry):
        r0 = pl.multiple_of(g * SUBLANES, SUBLANES)
        a = a_scr[pl.ds(r0, SUBLANES), :]
        u = u_scr[pl.ds(r0, SUBLANES), :]
        for s in (1, 2, 4):
            m = row >= s
            u = u + a * jnp.where(m, pltpu.roll(u, s, 0), 0.0)
            a = a * jnp.where(m, pltpu.roll(a, s, 0), 1.0)
        h = u + a * carry
        hf_ref[0, pl.ds(r0, SUBLANES), :] = h
        return h[SUBLANES - 1:SUBLANES, :]

    cf_scr[...] = lax.fori_loop(0, ng, fwd_body, cf_scr[...], unroll=4)

    gates(xb_ref, xbp_ref, xbn_ref, nt - 1 - j, 1)

    def bwd_body(gi, carry):
        g = ng - 1 - gi
        r0 = pl.multiple_of(g * SUBLANES, SUBLANES)
        a = a_scr[pl.ds(r0, SUBLANES), :]
        u = u_scr[pl.ds(r0, SUBLANES), :]
        for s in (1, 2, 4):
            m = row < SUBLANES - s
            u = u + a * jnp.where(m, pltpu.roll(u, SUBLANES - s, 0), 0.0)
            a = a * jnp.where(m, pltpu.roll(a, SUBLANES - s, 0), 1.0)
        h = u + a * carry
        hb_ref[0, pl.ds(r0, SUBLANES), :] = h
        return h[0:1, :]

    cr_scr[...] = lax.fori_loop(0, ng, bwd_body, cr_scr[...], unroll=4)


def _lru(z3, conv_w, conv_b, w_a_bf, b_a, w_x_bf, b_x, lam, tt):
    B, S, _ = z3.shape
    C = conv_w.shape[-1]
    nt = S // tt
    tb = tt // SUBLANES
    nb8 = S // SUBLANES

    def cur(jt):
        return lambda b, j: (b, jt(j), 0)

    def prev(jt):
        return lambda b, j: (b, jnp.maximum(jt(j) * tb - 1, 0), 0)

    def nxt(jt):
        return lambda b, j: (b, jnp.minimum((jt(j) + 1) * tb, nb8 - 1), 0)

    fw = lambda j: j
    bw = lambda j: nt - 1 - j
    full = lambda shape: pl.BlockSpec(shape, lambda b, j: (0,) * len(shape))
    return pl.pallas_call(
        functools.partial(_lru_kernel, tt=tt, nt=nt),
        grid=(B, nt),
        in_specs=[
            pl.BlockSpec((1, tt, C), cur(fw)),
            pl.BlockSpec((1, SUBLANES, C), prev(fw)),
            pl.BlockSpec((1, SUBLANES, C), nxt(fw)),
            pl.BlockSpec((1, tt, C), cur(bw)),
            pl.BlockSpec((1, SUBLANES, C), prev(bw)),
            pl.BlockSpec((1, SUBLANES, C), nxt(bw)),
            full((CONV_WIDTH, C)), full((1, C)),
            full(w_a_bf.shape), full((2, C)), full(w_x_bf.shape), full((2, C)), full((2, C)),
        ],
        out_specs=[
            pl.BlockSpec((1, tt, C), cur(fw)),
            pl.BlockSpec((1, tt, C), cur(bw)),
        ],
        out_shape=[jax.ShapeDtypeStruct((B, S, C), F32), jax.ShapeDtypeStruct((B, S, C), F32)],
        scratch_shapes=[
            pltpu.VMEM((tt, C), F32),
            pltpu.VMEM((tt, C), F32),
            pltpu.VMEM((1, C), F32),
            pltpu.VMEM((1, C), F32),
        ],
        compiler_params=_params(("parallel", "arbitrary")),
    )(z3, z3, z3, z3, z3, z3, conv_w, conv_b.reshape(1, C), w_a_bf, b_a, w_x_bf, b_x, lam)


def _gmlp_kernel(u_ref, v_ref, lng_ref, lnb_ref, ws_ref, bs_ref, o_ref, *, tc):
    v = v_ref[...]
    mu = jnp.mean(v, axis=-1, keepdims=True)
    vc = v - mu
    var = jnp.mean(vc * vc, axis=-1, keepdims=True)
    vb = (vc * lax.rsqrt(var + EPS) * lng_ref[...] + lnb_ref[...]).astype(BF16)
    hd = v.shape[-1] // GMLP_HEADS
    for c in range(tc // CHUNK):
        rows = slice(c * CHUNK, (c + 1) * CHUNK)
        mixed = jnp.concatenate(
            [jnp.dot(ws_ref[h], vb[rows, h * hd:(h + 1) * hd], preferred_element_type=F32)
             for h in range(GMLP_HEADS)], axis=1)
        o_ref[rows, :] = (u_ref[rows, :] * (mixed + bs_ref[...])).astype(BF16)


def _gmlp(z4, ln_g, ln_b, w_s_bf, bs_full, tc):
    M = z4.shape[0]
    C = ln_g.shape[-1]
    return pl.pallas_call(
        functools.partial(_gmlp_kernel, tc=tc),
        grid=(M // tc,),
        in_specs=[
            pl.BlockSpec((tc, C), lambda i: (i, 2)),
            pl.BlockSpec((tc, C), lambda i: (i, 3)),
            pl.BlockSpec((1, C), lambda i: (0, 0)),
            pl.BlockSpec((1, C), lambda i: (0, 0)),
            pl.BlockSpec(w_s_bf.shape, lambda i: (0, 0, 0)),
            pl.BlockSpec((CHUNK, C), lambda i: (0, 0)),
        ],
        out_specs=pl.BlockSpec((tc, C), lambda i: (i, 0)),
        out_shape=jax.ShapeDtypeStruct((M, C), BF16),
        compiler_params=_params(("parallel",)),
    )(z4, z4, ln_g.reshape(1, C), ln_b.reshape(1, C), w_s_bf, bs_full)


def _out_proj_kernel(hf_ref, hb_ref, gr_ref, bo_ref, x_ref, w_ref, g_ref, h1_ref, n2_ref):
    ca = hf_ref.shape[-1]
    a_out = ((hf_ref[...] + hb_ref[...]) * gr_ref[...]).astype(BF16)
    acc = jnp.dot(a_out, w_ref[0:ca, :], preferred_element_type=F32)
    acc = acc + jnp.dot(bo_ref[...], w_ref[ca:, :], preferred_element_type=F32)
    h1 = x_ref[...] + acc
    h1_ref[...] = h1
    ms = jnp.mean(h1 * h1, axis=-1, keepdims=True)
    n2_ref[...] = (h1 * lax.rsqrt(ms + EPS) * g_ref[...]).astype(BF16)


def _out_proj(hf, hb, z4, b_out, x2, w_bf, g, tm):
    M, D = x2.shape
    ca = hf.shape[-1]
    cb = b_out.shape[-1]
    return pl.pallas_call(
        _out_proj_kernel,
        grid=(M // tm,),
        in_specs=[
            pl.BlockSpec((tm, ca), lambda i: (i, 0)),
            pl.BlockSpec((tm, ca), lambda i: (i, 0)),
            pl.BlockSpec((tm, ca), lambda i: (i, 1)),
            pl.BlockSpec((tm, cb), lambda i: (i, 0)),
            pl.BlockSpec((tm, D), lambda i: (i, 0)),
            pl.BlockSpec(w_bf.shape, lambda i: (0, 0)),
            pl.BlockSpec((1, D), lambda i: (0, 0)),
        ],
        out_specs=[pl.BlockSpec((tm, D), lambda i: (i, 0)), pl.BlockSpec((tm, D), lambda i: (i, 0))],
        out_shape=[jax.ShapeDtypeStruct((M, D), F32), jax.ShapeDtypeStruct((M, D), BF16)],
        compiler_params=_params(("parallel",)),
    )(hf, hb, z4, b_out, x2, w_bf, g.reshape(1, D))


ROUTE_T = LANES
PLANE = ROUTE_T + SUBLANES


def _batcher_pairs(n):
    pairs = []

    def merge(lo, m, r):
        step = r * 2
        if step < m:
            merge(lo, m, step)
            merge(lo + r, m, step)
            pairs.extend((i, i + r) for i in range(lo + r, lo + m - r, step))
        else:
            pairs.append((lo, lo + r))

    def sort(lo, m):
        if m > 1:
            sort(lo, m // 2)
            sort(lo + m // 2, m // 2)
            merge(lo, m, 1)

    sort(0, n)
    return pairs


def _beats(x, y):
    return (x[0] > y[0]) | ((x[0] == y[0]) & (x[1] < y[1]))


def _best(x, y):
    g = _beats(x, y)
    return (jnp.maximum(x[0], y[0]),) + tuple(jnp.where(g, p, q) for p, q in zip(x[1:], y[1:]))


def _compare_exchange(x, y):
    g = _beats(x, y)
    hi = (jnp.maximum(x[0], y[0]),) + tuple(jnp.where(g, p, q) for p, q in zip(x[1:], y[1:]))
    lo = (jnp.minimum(x[0], y[0]),) + tuple(jnp.where(g, q, p) for p, q in zip(x[1:], y[1:]))
    return hi, lo


def _sort_desc(items):
    items = list(items)
    for i, j in _batcher_pairs(len(items)):
        items[i], items[j] = _compare_exchange(items[i], items[j])
    return items


def _bitonic_merge(items):
    items = list(items)
    n = len(items)
    d = n // 2
    while d >= 1:
        for i in range(n):
            if i & d == 0:
                items[i], items[i + d] = _compare_exchange(items[i], items[i + d])
        d //= 2
    return items


def _merge_top(x, y):
    n = len(x)
    return _bitonic_merge([_best(x[k], y[n - 1 - k]) for k in range(n)])


def _merge_full(x, y):
    return _bitonic_merge(list(x) + list(reversed(y)))


def _route_kernel(n_ref, wq_ref, keys_ref, g_ref, q_scr, s_scr, sv_scr, si_scr, w_scr, i_scr, j_scr,
                  wt_scr, it_scr, jt_scr, gs_scr):
    T = ROUTE_T
    K = PEER_TOPK
    NK = PEER_KEYS
    q_scr[...] = jnp.dot(n_ref[...], wq_ref[...], preferred_element_type=F32).astype(BF16)

    H = PEER_HEADS
    row8 = lax.broadcasted_iota(jnp.int32, (SUBLANES, T), 0).astype(F32)

    for u in range(2 * H):
        s_scr[u * NK:(u + 1) * NK, :] = lax.dot_general(
            keys_ref[u], q_scr[:, u * NK:(u + 1) * NK], _NT, preferred_element_type=F32)

    def unit_body(u, _):
        c0 = pl.multiple_of(u * NK, NK)
        items = [(s_scr[pl.ds(c0 + b * SUBLANES, SUBLANES), :], row8 + float(b * SUBLANES))
                 for b in range(NK // SUBLANES)]
        items = _sort_desc(items)
        half = jnp.bitwise_and(u, 1)
        head = jnp.right_shift(u, 1)
        for k, (v, i) in enumerate(items):
            r0 = pl.multiple_of(((half * K + k) * H + head) * SUBLANES, SUBLANES)
            sv_scr[pl.ds(r0, SUBLANES), :] = v
            si_scr[pl.ds(r0, SUBLANES), :] = i
        return 0

    lax.fori_loop(0, 2 * H, unit_body, 0)

    def half_top(half):
        def residue_list(r):
            starts = [(half * K + k) * H * SUBLANES + r for k in range(K)]
            return (tuple(sv_scr[pl.ds(st, H, stride=SUBLANES), :] for st in starts),
                    tuple(si_scr[pl.ds(st, H, stride=SUBLANES), :] for st in starts))

        def merge_body(r, acc):
            new = residue_list(r)
            out = _merge_top(list(zip(*acc)), list(zip(*new)))
            return tuple(x[0] for x in out), tuple(x[1] for x in out)

        vals, idxs = lax.fori_loop(1, SUBLANES, merge_body, residue_list(0))
        return list(zip(vals, idxs))

    a = half_top(0)
    b = half_top(1)

    def pair(p, r):
        code = jnp.full((H, T), float(p * K + r), F32)
        return (a[p][0] + b[r][0], code, a[p][1] * float(NK) + b[r][1])

    rows = [[pair(p, r) for r in range(K // (p + 1))] for p in range(K)]
    singles = [rows[p][0] for p in range(8, K)]
    l1 = _merge_full(rows[1], singles)
    l2 = _sort_desc(rows[2] + rows[3] + rows[4] + rows[5] + rows[6])
    top = _merge_top(rows[0], l1)
    top = _merge_top(top, l2)
    top[K - 2] = _best(top[K - 2], rows[7][1])
    top[K - 1] = _best(top[K - 1], rows[7][0])
    top = _bitonic_merge(top)

    es = [jnp.exp(t[0] - top[0][0]) for t in top]
    z = es[0]
    for e in es[1:]:
        z = z + e
    inv = 0.5 / z
    for k in range(K):
        ident = top[k][2]
        ii = jnp.floor(ident * (1.0 / NK))
        w_scr[k * H:(k + 1) * H, :] = es[k] * inv
        i_scr[k * H:(k + 1) * H, :] = ii
        j_scr[k * H:(k + 1) * H, :] = ident - ii * float(NK)

    wt_scr[...] = w_scr[...].T
    it_scr[...] = i_scr[...].T
    jt_scr[...] = j_scr[...].T

    sub = lax.broadcasted_iota(jnp.int32, (NK, PEER_HEADS * K), 0).astype(F32)

    zero_blk = jnp.zeros((PEER_HEADS * K, NK), BF16)

    def factors(t):
        wrow = wt_scr[pl.ds(t, 1), :]
        irow = it_scr[pl.ds(t, 1), :]
        jrow = jt_scr[pl.ds(t, 1), :]
        p1 = jnp.where(sub == irow, wrow, 0.0).astype(BF16)
        p2 = jnp.where(sub == jrow, 1.0, 0.0).astype(BF16).T
        return p1, p2

    def pair_body(q, _):
        t0 = 2 * q
        p1a, p2a = factors(t0)
        p1b, p2b = factors(t0 + 1)
        lhs = jnp.concatenate([p1a, p1b], axis=1)
        rhs = jnp.concatenate([jnp.concatenate([p2a, zero_blk], axis=1),
                               jnp.concatenate([zero_blk, p2b], axis=1)], axis=0)
        g = jnp.dot(lhs, rhs, preferred_element_type=F32)
        for blk in range(NK // SUBLANES):
            rows = slice(blk * SUBLANES, (blk + 1) * SUBLANES)
            base = blk * SUBLANES * PLANE + t0
            gs_scr[pl.ds(base, SUBLANES, stride=PLANE), :] = g[rows, 0:NK]
            gs_scr[pl.ds(base + 1, SUBLANES, stride=PLANE), :] = g[rows, NK:2 * NK]
        return 0

    lax.fori_loop(0, T // 2, pair_body, 0, unroll=True)

    for i in range(NK):
        g_ref[:, i * NK:(i + 1) * NK] = gs_scr[i * PLANE:i * PLANE + T, :].astype(BF16)


def _route(n2, wq_bf, keys_bf):
    M, D = n2.shape
    Q = wq_bf.shape[1]
    T = ROUTE_T
    NK = PEER_KEYS
    HK = PEER_HEADS * PEER_TOPK
    return pl.pallas_call(
        _route_kernel,
        grid=(M // T,),
        in_specs=[
            pl.BlockSpec((T, D), lambda i: (i, 0)),
            pl.BlockSpec((D, Q), lambda i: (0, 0)),
            pl.BlockSpec(keys_bf.shape, lambda i: (0, 0, 0)),
        ],
        out_specs=pl.BlockSpec((T, NK * NK), lambda i: (i, 0)),
        out_shape=jax.ShapeDtypeStruct((M, NK * NK), BF16),
        scratch_shapes=[
            pltpu.VMEM((T, Q), BF16),
            pltpu.VMEM((2 * PEER_HEADS * NK, T), F32),
            pltpu.VMEM((2 * HK * SUBLANES, T), F32), pltpu.VMEM((2 * HK * SUBLANES, T), F32),
            pltpu.VMEM((HK, T), F32), pltpu.VMEM((HK, T), F32), pltpu.VMEM((HK, T), F32),
            pltpu.VMEM((T, HK), F32), pltpu.VMEM((T, HK), F32), pltpu.VMEM((T, HK), F32),
            pltpu.VMEM((NK * PLANE, NK), F32),
        ],
        compiler_params=_params(("parallel",)),
    )(n2, wq_bf, keys_bf)


def _expert_kernel(n_ref, u_ref, v_ref, g_ref, h_ref, fg_ref, o_ref, *, sub, final_norm):
    k = pl.program_id(1)

    @pl.when(k == 0)
    def _():
        o_ref[...] = jnp.zeros_like(o_ref)

    n = n_ref[...]
    te = u_ref.shape[0]
    acc = None
    for c in range(te // sub):
        cs = slice(c * sub, (c + 1) * sub)
        act = lax.dot_general(n, u_ref[cs, :], _NT, preferred_element_type=F32)
        a = _twice_gelu(act.astype(BF16)) * g_ref[:, cs]
        part = jnp.dot(a, v_ref[cs, :], preferred_element_type=F32)
        acc = part if acc is None else acc + part
    o_ref[...] += acc
    hs = h_ref.shape[0]
    r0 = pl.multiple_of(k * hs, hs)
    o_ref[pl.ds(r0, hs), :] += h_ref[...]

    if final_norm:
        @pl.when(k == pl.num_programs(1) - 1)
        def _():
            h2 = o_ref[...]
            ms = jnp.mean(h2 * h2, axis=-1, keepdims=True)
            o_ref[...] = h2 * lax.rsqrt(ms + EPS) * fg_ref[...]


def _experts(n2, u_bf, v_bf, gates, h1, fg, tt, te, sub, final_norm):
    M, D = n2.shape
    nk = u_bf.shape[0] // te
    hs = tt // nk
    assert hs * nk == tt and hs % SUBLANES == 0
    return pl.pallas_call(
        functools.partial(_expert_kernel, sub=sub, final_norm=final_norm),
        grid=(M // tt, nk),
        in_specs=[
            pl.BlockSpec((tt, D), lambda i, k: (i, 0)),
            pl.BlockSpec((te, D), lambda i, k: (k, 0)),
            pl.BlockSpec((te, D), lambda i, k: (k, 0)),
            pl.BlockSpec((tt, te), lambda i, k: (i, k)),
            pl.BlockSpec((hs, D), lambda i, k: (i * nk + k, 0)),
            pl.BlockSpec((1, D), lambda i, k: (0, 0)),
        ],
        out_specs=pl.BlockSpec((tt, D), lambda i, k: (i, 0)),
        out_shape=jax.ShapeDtypeStruct((M, D), F32),
        compiler_params=_params(("parallel", "arbitrary"), EXPERTS_VMEM_LIMIT),
    )(n2, u_bf, v_bf, gates, h1, fg.reshape(1, D))


def _tile(n, want):
    t = min(n, want)
    assert n % t == 0, (n, t)
    return t


def kernel(x, mix_norm_g, w_in, conv_w, conv_b, lru_w_a, lru_b_a, lru_w_x, lru_b_x, lru_lambda,
           gmlp_ln_g, gmlp_ln_b, gmlp_w_s, gmlp_b_s, w_out, ffn_norm_g, peer_w_q, peer_sub_keys,
           peer_u, peer_v, final_norm_g):
    B, S, D = x.shape
    M = B * S
    depth = w_in.shape[0]
    assert S % CHUNK == 0 and M % ROUTE_T == 0
    h = x.reshape(M, D)
    for l in range(depth):
        z4 = _mix_in(h, mix_norm_g[l], w_in[l].astype(BF16), _tile(M, 1024))
        cw = z4.shape[1] // 4
        hf, hb = _lru(z4.reshape(B, S, 4 * cw), conv_w[l], conv_b[l],
                      (0.5 * lru_w_a[l]).astype(BF16), 0.5 * lru_b_a[l],
                      (0.5 * lru_w_x[l]).astype(BF16), 0.5 * lru_b_x[l],
                      lru_lambda[l], _tile(S, 512))
        bs_full = jnp.repeat(gmlp_b_s[l].T, cw // GMLP_HEADS, axis=1)
        b_out = _gmlp(z4, gmlp_ln_g[l], gmlp_ln_b[l], gmlp_w_s[l].astype(BF16), bs_full, _tile(M, 1024))
        h1, n2 = _out_proj(hf.reshape(M, cw), hb.reshape(M, cw), z4, b_out, h,
                           w_out[l].astype(BF16), ffn_norm_g[l], _tile(M, 512))
        keys = peer_sub_keys[l].reshape(PEER_HEADS * 2, PEER_KEYS, -1).astype(BF16)
        gates = _route(n2, peer_w_q[l].astype(BF16), keys)
        h = _experts(n2, peer_u[l].astype(BF16), peer_v[l].astype(BF16), gates, h1, final_norm_g,
                     _tile(M, 1024), 1024, 512, l == depth - 1)
    return h.reshape(B, S, D)
```

```python
import functools
from typing import NamedTuple

import jax
import jax.numpy as jnp
from jax import lax
from jax.experimental import pallas as pl
from jax.experimental.pallas import tpu as pltpu

F32 = jnp.float32
BF16 = jnp.bfloat16

EPS = 1e-6
LRU_C = 8.0
LRU_HEADS = 8
CONV_WIDTH = 4
CONV_LEFT = 2
GMLP_HEADS = 8
CHUNK = 128
PEER_HEADS = 8
PEER_KEYS = 128
PEER_TOPK = 16

LANES = 128
SUBLANES = 8
VMEM_LIMIT = 56 * 1024 * 1024

_NT = (((1,), (1,)), ((), ()))


def _gelu_tanh_arg(x):
    c = 0.7978845608028654
    return x * (c + (c * 0.044715) * (x * x))


def _gelu(x):
    return x * (0.5 + 0.5 * jnp.tanh(_gelu_tanh_arg(x)))


EXPERTS_VMEM_LIMIT = 60000 * 1024


def _params(sem, vmem_limit=VMEM_LIMIT):
    return pltpu.CompilerParams(dimension_semantics=sem, vmem_limit_bytes=vmem_limit)


def _mix_in_kernel(x_ref, g_ref, w_ref, o_ref, n_scr):
    j = pl.program_id(1)

    @pl.when(j == 0)
    def _():
        x = x_ref[...]
        ms = jnp.mean(x * x, axis=-1, keepdims=True)
        n_scr[...] = (x * lax.rsqrt(ms + EPS) * g_ref[...]).astype(BF16)

    acc = jnp.dot(n_scr[...], w_ref[...], preferred_element_type=F32)

    @pl.when(j == 0)
    def _():
        o_ref[...] = acc

    @pl.when(j > 0)
    def _():
        o_ref[...] = _gelu(acc)


def _mix_in(x2, g, w_bf, tm):
    M, D = x2.shape
    N = w_bf.shape[1]
    tn = N // 4
    return pl.pallas_call(
        _mix_in_kernel,
        grid=(M // tm, 4),
        in_specs=[
            pl.BlockSpec((tm, D), lambda i, j: (i, 0)),
            pl.BlockSpec((1, D), lambda i, j: (0, 0)),
            pl.BlockSpec((D, tn), lambda i, j: (0, j)),
        ],
        out_specs=pl.BlockSpec((tm, tn), lambda i, j: (i, j)),
        out_shape=jax.ShapeDtypeStruct((M, N), F32),
        scratch_shapes=[pltpu.VMEM((tm, D), BF16)],
        compiler_params=_params(("parallel", "arbitrary")),
    )(x2, g.reshape(1, D), w_bf)


def _lru_kernel(xf_ref, xfp_ref, xfn_ref, xb_ref, xbp_ref, xbn_ref,
                cw_ref, cb_ref, wa_ref, ba_ref, wx_ref, bx_ref, lam_ref,
                hf_ref, hb_ref,
                a_scr, u_scr, cf_scr, cr_scr, *, tt, nt):
    j = pl.program_id(1)
    C = xf_ref.shape[-1]
    hd = C // LRU_HEADS

    @pl.when(j == 0)
    def _():
        cf_scr[...] = jnp.zeros_like(cf_scr)
        cr_scr[...] = jnp.zeros_like(cr_scr)

    def gates(cur_ref, prev_ref, next_ref, jt, d):
        xe = jnp.concatenate([jnp.where(jt > 0, prev_ref[0], 0.0), cur_ref[0],
                              jnp.where(jt < nt - 1, next_ref[0], 0.0)], axis=0)
        rows = tt + 2 * SUBLANES
        xc = cb_ref[...] + jnp.zeros((tt, C), F32)
        for k in range(CONV_WIDTH):
            shift = (CONV_LEFT - k) % rows
            xs = pltpu.roll(xe, shift, 0) if shift else xe
            xc = xc + xs[SUBLANES:SUBLANES + tt, :] * cw_ref[k:k + 1, :]
        xcb = xc.astype(BF16)
        ra = jnp.concatenate(
            [jnp.dot(xcb[:, h * hd:(h + 1) * hd], wa_ref[d, h], preferred_element_type=F32)
             for h in range(LRU_HEADS)], axis=1) + ba_ref[d:d + 1, :]
        rx = jnp.concatenate(
            [jnp.dot(xcb[:, h * hd:(h + 1) * hd], wx_ref[d, h], preferred_element_type=F32)
             for h in range(LRU_HEADS)], axis=1) + bx_ref[d:d + 1, :]
        ta = jnp.tanh(ra)
        tx = jnp.tanh(rx)
        nl = -lam_ref[d:d + 1, :]
        sp = jnp.maximum(nl, 0.0) + jnp.log(1.0 + jnp.exp(-jnp.abs(nl)))
        c = (-0.5 * LRU_C * 1.4426950408889634) * sp
        a = jnp.exp2(c + c * ta)
        a_scr[...] = a
        y = 1.0 - a * a
        u_scr[...] = jnp.where(y > 0.0, (0.5 * y) * lax.rsqrt(y), 0.0) * ((1.0 + tx) * xc)

    row = lax.broadcasted_iota(jnp.int32, (SUBLANES, C), 0)
    ng = tt // SUBLANES

    gates(xf_ref, xfp_ref, xfn_ref, j, 0)

    def fwd_body(g, carry):
        r0 = pl.multiple_of(g * SUBLANES, SUBLANES)
        a = a_scr[pl.ds(r0, SUBLANES), :]
        u = u_scr[pl.ds(r0, SUBLANES), :]
        for s in (1, 2, 4):
            m = row >= s
            u = u + a * jnp.where(m, pltpu.roll(u, s, 0), 0.0)
            a = a * jnp.where(m, pltpu.roll(a, s, 0), 1.0)
        h = u + a * carry
        hf_ref[0, pl.ds(r0, SUBLANES), :] = h
        return h[SUBLANES - 1:SUBLANES, :]

    cf_scr[...] = lax.fori_loop(0, ng, fwd_body, cf_scr[...], unroll=4)

    gates(xb_ref, xbp_ref, xbn_ref, nt - 1 - j, 1)

    def bwd_body(gi, carry):
        g = ng - 1 - gi
        r0 = pl.multiple_of(g * SUBLANES, SUBLANES)
        a = a_scr[pl.ds(r0, SUBLANES), :]
        u = u_scr[pl.ds(r0, SUBLANES), :]
        for s in (1, 2, 4):
            m = row < SUBLANES - s
            u = u + a * jnp.where(m, pltpu.roll(u, SUBLANES - s, 0), 0.0)
            a = a * jnp.where(m, pltpu.roll(a, SUBLANES - s, 0), 1.0)
        h = u + a * carry
        hb_ref[0, pl.ds(r0, SUBLANES), :] = h
        return h[0:1, :]

    cr_scr[...] = lax.fori_loop(0, ng, bwd_body, cr_scr[...], unroll=4)


def _lru(z3, conv_w, conv_b, w_a_bf, b_a, w_x_bf, b_x, lam, tt):
    B, S, _ = z3.shape
    C = conv_w.shape[-1]
    nt = S // tt
    tb = tt // SUBLANES
    nb8 = S // SUBLANES

    def cur(jt):
        return lambda b, j: (b, jt(j), 0)

    def prev(jt):
        return lambda b, j: (b, jnp.maximum(jt(j) * tb - 1, 0), 0)

    def nxt(jt):
        return lambda b, j: (b, jnp.minimum((jt(j) + 1) * tb, nb8 - 1), 0)

    fw = lambda j: j
    bw = lambda j: nt - 1 - j
    full = lambda shape: pl.BlockSpec(shape, lambda b, j: (0,) * len(shape))
    return pl.pallas_call(
        functools.partial(_lru_kernel, tt=tt, nt=nt),
        grid=(B, nt),
        in_specs=[
            pl.BlockSpec((1, tt, C), cur(fw)),
            pl.BlockSpec((1, SUBLANES, C), prev(fw)),
            pl.BlockSpec((1, SUBLANES, C), nxt(fw)),
            pl.BlockSpec((1, tt, C), cur(bw)),
            pl.BlockSpec((1, SUBLANES, C), prev(bw)),
            pl.BlockSpec((1, SUBLANES, C), nxt(bw)),
            full((CONV_WIDTH, C)), full((1, C)),
            full(w_a_bf.shape), full((2, C)), full(w_x_bf.shape), full((2, C)), full((2, C)),
        ],
        out_specs=[
            pl.BlockSpec((1, tt, C), cur(fw)),
            pl.BlockSpec((1, tt, C), cur(bw)),
        ],
        out_shape=[jax.ShapeDtypeStruct((B, S, C), F32), jax.ShapeDtypeStruct((B, S, C), F32)],
        scratch_shapes=[
            pltpu.VMEM((tt, C), F32),
            pltpu.VMEM((tt, C), F32),
            pltpu.VMEM((1, C), F32),
            pltpu.VMEM((1, C), F32),
        ],
        compiler_params=_params(("parallel", "arbitrary")),
    )(z3, z3, z3, z3, z3, z3, conv_w, conv_b.reshape(1, C), w_a_bf, b_a, w_x_bf, b_x, lam)


def _gmlp_kernel(u_ref, v_ref, lng_ref, lnb_ref, ws_ref, bs_ref, o_ref, *, tc):
    v = v_ref[...]
    mu = jnp.mean(v, axis=-1, keepdims=True)
    vc = v - mu
    var = jnp.mean(vc * vc, axis=-1, keepdims=True)
    vb = (vc * lax.rsqrt(var + EPS) * lng_ref[...] + lnb_ref[...]).astype(BF16)
    hd = v.shape[-1] // GMLP_HEADS
    for c in range(tc // CHUNK):
        rows = slice(c * CHUNK, (c + 1) * CHUNK)
        mixed = jnp.concatenate(
            [jnp.dot(ws_ref[h], vb[rows, h * hd:(h + 1) * hd], preferred_element_type=F32)
             for h in range(GMLP_HEADS)], axis=1)
        o_ref[rows, :] = (u_ref[rows, :] * (mixed + bs_ref[...])).astype(BF16)


def _gmlp(z4, ln_g, ln_b, w_s_bf, bs_full, tc):
    M = z4.shape[0]
    C = ln_g.shape[-1]
    return pl.pallas_call(
        functools.partial(_gmlp_kernel, tc=tc),
        grid=(M // tc,),
        in_specs=[
            pl.BlockSpec((tc, C), lambda i: (i, 2)),
            pl.BlockSpec((tc, C), lambda i: (i, 3)),
            pl.BlockSpec((1, C), lambda i: (0, 0)),
            pl.BlockSpec((1, C), lambda i: (0, 0)),
            pl.BlockSpec(w_s_bf.shape, lambda i: (0, 0, 0)),
            pl.BlockSpec((CHUNK, C), lambda i: (0, 0)),
        ],
        out_specs=pl.BlockSpec((tc, C), lambda i: (i, 0)),
        out_shape=jax.ShapeDtypeStruct((M, C), BF16),
        compiler_params=_params(("parallel",)),
    )(z4, z4, ln_g.reshape(1, C), ln_b.reshape(1, C), w_s_bf, bs_full)


def _out_proj_kernel(hf_ref, hb_ref, gr_ref, bo_ref, x_ref, w_ref, g_ref, h1_ref, n2_ref):
    ca = hf_ref.shape[-1]
    a_out = ((hf_ref[...] + hb_ref[...]) * gr_ref[...]).astype(BF16)
    acc = jnp.dot(a_out, w_ref[0:ca, :], preferred_element_type=F32)
    acc = acc + jnp.dot(bo_ref[...], w_ref[ca:, :], preferred_element_type=F32)
    h1 = x_ref[...] + acc
    h1_ref[...] = h1
    ms = jnp.mean(h1 * h1, axis=-1, keepdims=True)
    n2_ref[...] = (h1 * lax.rsqrt(ms + EPS) * g_ref[...]).astype(BF16)


def _out_proj(hf, hb, z4, b_out, x2, w_bf, g, tm):
    M, D = x2.shape
    ca = hf.shape[-1]
    cb = b_out.shape[-1]
    return pl.pallas_call(
        _out_proj_kernel,
        grid=(M // tm,),
        in_specs=[
            pl.BlockSpec((tm, ca), lambda i: (i, 0)),
            pl.BlockSpec((tm, ca), lambda i: (i, 0)),
            pl.BlockSpec((tm, ca), lambda i: (i, 1)),
            pl.BlockSpec((tm, cb), lambda i: (i, 0)),
            pl.BlockSpec((tm, D), lambda i: (i, 0)),
            pl.BlockSpec(w_bf.shape, lambda i: (0, 0)),
            pl.BlockSpec((1, D), lambda i: (0, 0)),
        ],
        out_specs=[pl.BlockSpec((tm, D), lambda i: (i, 0)), pl.BlockSpec((tm, D), lambda i: (i, 0))],
        out_shape=[jax.ShapeDtypeStruct((M, D), F32), jax.ShapeDtypeStruct((M, D), BF16)],
        compiler_params=_params(("parallel",)),
    )(hf, hb, z4, b_out, x2, w_bf, g.reshape(1, D))


ROUTE_T = LANES
PLANE = ROUTE_T + SUBLANES


def _batcher_pairs(n):
    pairs = []

    def merge(lo, m, r):
        step = r * 2
        if step < m:
            merge(lo, m, step)
            merge(lo + r, m, step)
            pairs.extend((i, i + r) for i in range(lo + r, lo + m - r, step))
        else:
            pairs.append((lo, lo + r))

    def sort(lo, m):
        if m > 1:
            sort(lo, m // 2)
            sort(lo + m // 2, m // 2)
            merge(lo, m, 1)

    sort(0, n)
    return pairs


def _beats(x, y):
    return (x[0] > y[0]) | ((x[0] == y[0]) & (x[1] < y[1]))


def _best(x, y):
    g = _beats(x, y)
    return (jnp.maximum(x[0], y[0]),) + tuple(jnp.where(g, p, q) for p, q in zip(x[1:], y[1:]))


def _compare_exchange(x, y):
    g = _beats(x, y)
    hi = (jnp.maximum(x[0], y[0]),) + tuple(jnp.where(g, p, q) for p, q in zip(x[1:], y[1:]))
    lo = (jnp.minimum(x[0], y[0]),) + tuple(jnp.where(g, q, p) for p, q in zip(x[1:], y[1:]))
    return hi, lo


def _sort_desc(items):
    items = list(items)
    for i, j in _batcher_pairs(len(items)):
        items[i], items[j] = _compare_exchange(items[i], items[j])
    return items


def _bitonic_merge(items):
    items = list(items)
    n = len(items)
    d = n // 2
    while d >= 1:
        for i in range(n):
            if i & d == 0:
                items[i], items[i + d] = _compare_exchange(items[i], items[i + d])
        d //= 2
    return items


def _merge_top(x, y):
    n = len(x)
    return _bitonic_merge([_best(x[k], y[n - 1 - k]) for k in range(n)])


def _merge_full(x, y):
    return _bitonic_merge(list(x) + list(reversed(y)))


def _route_kernel(n_ref, wq_ref, keys_ref, g_ref, q_scr, s_scr, sv_scr, si_scr, w_scr, i_scr, j_scr,
                  wt_scr, it_scr, jt_scr, gs_scr):
    T = ROUTE_T
    K = PEER_TOPK
    NK = PEER_KEYS
    q_scr[...] = jnp.dot(n_ref[...], wq_ref[...], preferred_element_type=F32).astype(BF16)

    H = PEER_HEADS
    row8 = lax.broadcasted_iota(jnp.int32, (SUBLANES, T), 0).astype(F32)

    for u in range(2 * H):
        s_scr[u * NK:(u + 1) * NK, :] = lax.dot_general(
            keys_ref[u], q_scr[:, u * NK:(u + 1) * NK], _NT, preferred_element_type=F32)

    def unit_body(u, _):
        c0 = pl.multiple_of(u * NK, NK)
        items = [(s_scr[pl.ds(c0 + b * SUBLANES, SUBLANES), :], row8 + float(b * SUBLANES))
                 for b in range(NK // SUBLANES)]
        items = _sort_desc(items)
        half = jnp.bitwise_and(u, 1)
        head = jnp.right_shift(u, 1)
        for k, (v, i) in enumerate(items):
            r0 = pl.multiple_of(((half * K + k) * H + head) * SUBLANES, SUBLANES)
            sv_scr[pl.ds(r0, SUBLANES), :] = v
            si_scr[pl.ds(r0, SUBLANES), :] = i
        return 0

    lax.fori_loop(0, 2 * H, unit_body, 0)

    def half_top(half):
        def residue_list(r):
            starts = [(half * K + k) * H * SUBLANES + r for k in range(K)]
            return (tuple(sv_scr[pl.ds(st, H, stride=SUBLANES), :] for st in starts),
                    tuple(si_scr[pl.ds(st, H, stride=SUBLANES), :] for st in starts))

        def merge_body(r, acc):
            new = residue_list(r)
            out = _merge_top(list(zip(*acc)), list(zip(*new)))
            return tuple(x[0] for x in out), tuple(x[1] for x in out)

        vals, idxs = lax.fori_loop(1, SUBLANES, merge_body, residue_list(0))
        return list(zip(vals, idxs))

    a = half_top(0)
    b = half_top(1)

    def pair(p, r):
        code = jnp.full((H, T), float(p * K + r), F32)
        return (a[p][0] + b[r][0], code, a[p][1] * float(NK) + b[r][1])

    rows = [[pair(p, r) for r in range(K // (p + 1))] for p in range(K)]
    singles = [rows[p][0] for p in range(8, K)]
    l1 = _merge_full(rows[1], singles)
    l2 = _sort_desc(rows[2] + rows[3] + rows[4] + rows[5] + rows[6])
    top = _merge_top(rows[0], l1)
    top = _merge_top(top, l2)
    top[K - 2] = _best(top[K - 2], rows[7][1])
    top[K - 1] = _best(top[K - 1], rows[7][0])
    top = _bitonic_merge(top)

    es = [jnp.exp(t[0] - top[0][0]) for t in top]
    z = es[0]
    for e in es[1:]:
        z = z + e
    inv = 0.5 / z
    for k in range(K):
        ident = top[k][2]
        ii = jnp.floor(ident * (1.0 / NK))
        w_scr[k * H:(k + 1) * H, :] = es[k] * inv
        i_scr[k * H:(k + 1) * H, :] = ii
        j_scr[k * H:(k + 1) * H, :] = ident - ii * float(NK)

    wt_scr[...] = w_scr[...].T
    it_scr[...] = i_scr[...].T
    jt_scr[...] = j_scr[...].T

    sub = lax.broadcasted_iota(jnp.int32, (NK, PEER_HEADS * K), 0).astype(F32)

    zero_blk = jnp.zeros((PEER_HEADS * K, NK), BF16)

    def factors(t):
        wrow = wt_scr[pl.ds(t, 1), :]
        irow = it_scr[pl.ds(t, 1), :]
        jrow = jt_scr[pl.ds(t, 1), :]
        p1 = jnp.where(sub == irow, wrow, 0.0).astype(BF16)
        p2 = jnp.where(sub == jrow, 1.0, 0.0).astype(BF16).T
        return p1, p2

    def pair_body(q, _):
        t0 = 2 * q
        p1a, p2a = factors(t0)
        p1b, p2b = factors(t0 + 1)
        lhs = jnp.concatenate([p1a, p1b], axis=1)
        rhs = jnp.concatenate([jnp.concatenate([p2a, zero_blk], axis=1),
                               jnp.concatenate([zero_blk, p2b], axis=1)], axis=0)
        g = jnp.dot(lhs, rhs, preferred_element_type=F32)
        for blk in range(NK // SUBLANES):
            rows = slice(blk * SUBLANES, (blk + 1) * SUBLANES)
            base = blk * SUBLANES * PLANE + t0
            gs_scr[pl.ds(base, SUBLANES, stride=PLANE), :] = g[rows, 0:NK]
            gs_scr[pl.ds(base + 1, SUBLANES, stride=PLANE), :] = g[rows, NK:2 * NK]
        return 0

    lax.fori_loop(0, T // 2, pair_body, 0, unroll=True)

    for i in range(NK):
        g_ref[:, i * NK:(i + 1) * NK] = gs_scr[i * PLANE:i * PLANE + T, :].astype(BF16)


def _route(n2, wq_bf, keys_bf):
    M, D = n2.shape
    Q = wq_bf.shape[1]
    T = ROUTE_T
    NK = PEER_KEYS
    HK = PEER_HEADS * PEER_TOPK
    return pl.pallas_call(
        _route_kernel,
        grid=(M // T,),
        in_specs=[
            pl.BlockSpec((T, D), lambda i: (i, 0)),
            pl.BlockSpec((D, Q), lambda i: (0, 0)),
            pl.BlockSpec(keys_bf.shape, lambda i: (0, 0, 0)),
        ],
        out_specs=pl.BlockSpec((T, NK * NK), lambda i: (i, 0)),
        out_shape=jax.ShapeDtypeStruct((M, NK * NK), BF16),
        scratch_shapes=[
            pltpu.VMEM((T, Q), BF16),
            pltpu.VMEM((2 * PEER_HEADS * NK, T), F32),
            pltpu.VMEM((2 * HK * SUBLANES, T), F32), pltpu.VMEM((2 * HK * SUBLANES, T), F32),
            pltpu.VMEM((HK, T), F32), pltpu.VMEM((HK, T), F32), pltpu.VMEM((HK, T), F32),
            pltpu.VMEM((T, HK), F32), pltpu.VMEM((T, HK), F32), pltpu.VMEM((T, HK), F32),
            pltpu.VMEM((NK * PLANE, NK), F32),
        ],
        compiler_params=_params(("parallel",)),
    )(n2, wq_bf, keys_bf)


def _expert_kernel(n_ref, u_ref, v_ref, g_ref, h_ref, fg_ref, o_ref, *, sub, final_norm):
    k = pl.program_id(1)

    @pl.when(k == 0)
    def _():
        o_ref[...] = jnp.zeros_like(o_ref)

    n = n_ref[...]
    te = u_ref.shape[0]
    acc = None
    for c in range(te // sub):
        cs = slice(c * sub, (c + 1) * sub)
        act = lax.dot_general(n, u_ref[cs, :], _NT, preferred_element_type=F32)
        t = jnp.tanh(_gelu_tanh_arg(act).astype(BF16))
        a = act.astype(BF16) * (1.0 + t) * g_ref[:, cs]
        part = jnp.dot(a, v_ref[cs, :], preferred_element_type=F32)
        acc = part if acc is None else acc + part
    o_ref[...] += acc
    hs = h_ref.shape[0]
    r0 = pl.multiple_of(k * hs, hs)
    o_ref[pl.ds(r0, hs), :] += h_ref[...]

    if final_norm:
        @pl.when(k == pl.num_programs(1) - 1)
        def _():
            h2 = o_ref[...]
            ms = jnp.mean(h2 * h2, axis=-1, keepdims=True)
            o_ref[...] = h2 * lax.rsqrt(ms + EPS) * fg_ref[...]


def _experts(n2, u_bf, v_bf, gates, h1, fg, tt, te, sub, final_norm):
    M, D = n2.shape
    nk = u_bf.shape[0] // te
    hs = tt // nk
    assert hs * nk == tt and hs % SUBLANES == 0
    return pl.pallas_call(
        functools.partial(_expert_kernel, sub=sub, final_norm=final_norm),
        grid=(M // tt, nk),
        in_specs=[
            pl.BlockSpec((tt, D), lambda i, k: (i, 0)),
            pl.BlockSpec((te, D), lambda i, k: (k, 0)),
            pl.BlockSpec((te, D), lambda i, k: (k, 0)),
            pl.BlockSpec((tt, te), lambda i, k: (i, k)),
            pl.BlockSpec((hs, D), lambda i, k: (i * nk + k, 0)),
            pl.BlockSpec((1, D), lambda i, k: (0, 0)),
        ],
        out_specs=pl.BlockSpec((tt, D), lambda i, k: (i, 0)),
        out_shape=jax.ShapeDtypeStruct((M, D), F32),
        compiler_params=_params(("parallel", "arbitrary"), EXPERTS_VMEM_LIMIT),
    )(n2, u_bf, v_bf, gates, h1, fg.reshape(1, D))


def _tile(n, want):
    t = min(n, want)
    assert n % t == 0, (n, t)
    return t


class _Tiles(NamedTuple):
    mix_in_rows: int
    lru_steps: int
    out_proj_rows: int
    expert_tokens: int
    expert_tile: int
    expert_chunk: int


def _plan(M, S, E):
    expert_tile = _tile(E, 1024)
    return _Tiles(_tile(M, 1024), _tile(S, 512), _tile(M, 512), _tile(M, 1024), expert_tile,
                  _tile(expert_tile, 512))


def kernel(x, mix_norm_g, w_in, conv_w, conv_b, lru_w_a, lru_b_a, lru_w_x, lru_b_x, lru_lambda,
           gmlp_ln_g, gmlp_ln_b, gmlp_w_s, gmlp_b_s, w_out, ffn_norm_g, peer_w_q, peer_sub_keys,
           peer_u, peer_v, final_norm_g):
    B, S, D = x.shape
    M = B * S
    depth = w_in.shape[0]
    assert S % CHUNK == 0 and M % ROUTE_T == 0
    t = _plan(M, S, peer_u.shape[1])
    h = x.reshape(M, D)
    for l in range(depth):
        z4 = _mix_in(h, mix_norm_g[l], w_in[l].astype(BF16), t.mix_in_rows)
        cw = z4.shape[1] // 4
        hf, hb = _lru(z4.reshape(B, S, 4 * cw), conv_w[l], conv_b[l],
                      (0.5 * lru_w_a[l]).astype(BF16), 0.5 * lru_b_a[l],
                      (0.5 * lru_w_x[l]).astype(BF16), 0.5 * lru_b_x[l],
                      lru_lambda[l], t.lru_steps)
        bs_full = jnp.repeat(gmlp_b_s[l].T, cw // GMLP_HEADS, axis=1)
        b_out = _gmlp(z4, gmlp_ln_g[l], gmlp_ln_b[l], gmlp_w_s[l].astype(BF16), bs_full, t.mix_in_rows)
        h1, n2 = _out_proj(hf.reshape(M, cw), hb.reshape(M, cw), z4, b_out, h,
                           w_out[l].astype(BF16), ffn_norm_g[l], t.out_proj_rows)
        keys = peer_sub_keys[l].reshape(PEER_HEADS * 2, PEER_KEYS, -1).astype(BF16)
        gates = _route(n2, peer_w_q[l].astype(BF16), keys)
        h = _experts(n2, peer_u[l].astype(BF16), peer_v[l].astype(BF16), gates, h1, final_norm_g,
                     t.expert_tokens, t.expert_tile, t.expert_chunk, l == depth - 1)
    return h.reshape(B, S, D)
```

```python
import functools
from typing import NamedTuple

import jax
import jax.numpy as jnp
from jax import lax
from jax.experimental import pallas as pl
from jax.experimental.pallas import tpu as pltpu

F32 = jnp.float32
BF16 = jnp.bfloat16

EPS = 1e-6
LRU_C = 8.0
LRU_HEADS = 8
CONV_WIDTH = 4
CONV_LEFT = 2
GMLP_HEADS = 8
CHUNK = 128
PEER_HEADS = 8
PEER_KEYS = 128
PEER_TOPK = 16

LANES = 128
SUBLANES = 8
VMEM_LIMIT = 56 * 1024 * 1024

_NT = (((1,), (1,)), ((), ()))


def _gelu_tanh_arg(x):
    c = 0.7978845608028654
    return x * (c + (c * 0.044715) * (x * x))


def _gelu(x):
    return x * (0.5 + 0.5 * jnp.tanh(_gelu_tanh_arg(x)))


EXPERTS_VMEM_LIMIT = 60000 * 1024


def _params(sem, vmem_limit=VMEM_LIMIT):
    return pltpu.CompilerParams(dimension_semantics=sem, vmem_limit_bytes=vmem_limit)


def _mix_in_kernel(x_ref, g_ref, w_ref, o_ref, n_scr):
    j = pl.program_id(1)

    @pl.when(j == 0)
    def _():
        x = x_ref[...]
        ms = jnp.mean(x * x, axis=-1, keepdims=True)
        n_scr[...] = (x * lax.rsqrt(ms + EPS) * g_ref[...]).astype(BF16)

    acc = jnp.dot(n_scr[...], w_ref[...], preferred_element_type=F32)

    @pl.when(j == 0)
    def _():
        o_ref[...] = acc

    @pl.when(j > 0)
    def _():
        o_ref[...] = _gelu(acc)


def _mix_in(x2, g, w_bf, tm):
    M, D = x2.shape
    N = w_bf.shape[1]
    tn = N // 4
    return pl.pallas_call(
        _mix_in_kernel,
        grid=(M // tm, 4),
        in_specs=[
            pl.BlockSpec((tm, D), lambda i, j: (i, 0)),
            pl.BlockSpec((1, D), lambda i, j: (0, 0)),
            pl.BlockSpec((D, tn), lambda i, j: (0, j)),
        ],
        out_specs=pl.BlockSpec((tm, tn), lambda i, j: (i, j)),
        out_shape=jax.ShapeDtypeStruct((M, N), F32),
        scratch_shapes=[pltpu.VMEM((tm, D), BF16)],
        compiler_params=_params(("parallel", "arbitrary")),
    )(x2, g.reshape(1, D), w_bf)


def _lru_kernel(xf_ref, xfp_ref, xfn_ref, xb_ref, xbp_ref, xbn_ref,
                cw_ref, cb_ref, wa_ref, ba_ref, wx_ref, bx_ref, lam_ref,
                hf_ref, hb_ref,
                a_scr, u_scr, cf_scr, cr_scr, *, tt, nt):
    j = pl.program_id(1)
    C = xf_ref.shape[-1]
    hd = C // LRU_HEADS

    @pl.when(j == 0)
    def _():
        cf_scr[...] = jnp.zeros_like(cf_scr)
        cr_scr[...] = jnp.zeros_like(cr_scr)

    def gates(cur_ref, prev_ref, next_ref, jt, d):
        xe = jnp.concatenate([jnp.where(jt > 0, prev_ref[0], 0.0), cur_ref[0],
                              jnp.where(jt < nt - 1, next_ref[0], 0.0)], axis=0)
        rows = tt + 2 * SUBLANES
        xc = cb_ref[...] + jnp.zeros((tt, C), F32)
        for k in range(CONV_WIDTH):
            shift = (CONV_LEFT - k) % rows
            xs = pltpu.roll(xe, shift, 0) if shift else xe
            xc = xc + xs[SUBLANES:SUBLANES + tt, :] * cw_ref[k:k + 1, :]
        xcb = xc.astype(BF16)
        ra = jnp.concatenate(
            [jnp.dot(xcb[:, h * hd:(h + 1) * hd], wa_ref[d, h], preferred_element_type=F32)
             for h in range(LRU_HEADS)], axis=1) + ba_ref[d:d + 1, :]
        rx = jnp.concatenate(
            [jnp.dot(xcb[:, h * hd:(h + 1) * hd], wx_ref[d, h], preferred_element_type=F32)
             for h in range(LRU_HEADS)], axis=1) + bx_ref[d:d + 1, :]
        ta = jnp.tanh(ra)
        tx = jnp.tanh(rx)
        nl = -lam_ref[d:d + 1, :]
        sp = jnp.maximum(nl, 0.0) + jnp.log(1.0 + jnp.exp(-jnp.abs(nl)))
        c = (-0.5 * LRU_C * 1.4426950408889634) * sp
        a = jnp.exp2(c + c * ta)
        a_scr[...] = a
        y = 1.0 - a * a
        u_scr[...] = jnp.where(y > 0.0, (0.5 * y) * lax.rsqrt(y), 0.0) * ((1.0 + tx) * xc)

    row = lax.broadcasted_iota(jnp.int32, (SUBLANES, C), 0)
    ng = tt // SUBLANES

    gates(xf_ref, xfp_ref, xfn_ref, j, 0)

    def fwd_body(g, carry):
        r0 = pl.multiple_of(g * SUBLANES, SUBLANES)
        a = a_scr[pl.ds(r0, SUBLANES), :]
        u = u_scr[pl.ds(r0, SUBLANES), :]
        for s in (1, 2, 4):
            m = row >= s
            u = u + a * jnp.where(m, pltpu.roll(u, s, 0), 0.0)
            a = a * jnp.where(m, pltpu.roll(a, s, 0), 1.0)
        h = u + a * carry
        hf_ref[0, pl.ds(r0, SUBLANES), :] = h
        return h[SUBLANES - 1:SUBLANES, :]

    cf_scr[...] = lax.fori_loop(0, ng, fwd_body, cf_scr[...], unroll=4)

    gates(xb_ref, xbp_ref, xbn_ref, nt - 1 - j, 1)

    def bwd_body(gi, carry):
        g = ng - 1 - gi
        r0 = pl.multiple_of(g * SUBLANES, SUBLANES)
        a = a_scr[pl.ds(r0, SUBLANES), :]
        u = u_scr[pl.ds(r0, SUBLANES), :]
        for s in (1, 2, 4):
            m = row < SUBLANES - s
            u = u + a * jnp.where(m, pltpu.roll(u, SUBLANES - s, 0), 0.0)
            a = a * jnp.where(m, pltpu.roll(a, SUBLANES - s, 0), 1.0)
        h = u + a * carry
        hb_ref[0, pl.ds(r0, SUBLANES), :] = h
        return h[0:1, :]

    cr_scr[...] = lax.fori_loop(0, ng, bwd_body, cr_scr[...], unroll=4)


def _lru(z3, conv_w, conv_b, w_a_bf, b_a, w_x_bf, b_x, lam, tt):
    B, S, _ = z3.shape
    C = conv_w.shape[-1]
    nt = S // tt
    tb = tt // SUBLANES
    nb8 = S // SUBLANES

    def cur(jt):
        return lambda b, j: (b, jt(j), 0)

    def prev(jt):
        return lambda b, j: (b, jnp.maximum(jt(j) * tb - 1, 0), 0)

    def nxt(jt):
        return lambda b, j: (b, jnp.minimum((jt(j) + 1) * tb, nb8 - 1), 0)

    fw = lambda j: j
    bw = lambda j: nt - 1 - j
    full = lambda shape: pl.BlockSpec(shape, lambda b, j: (0,) * len(shape))
    return pl.pallas_call(
        functools.partial(_lru_kernel, tt=tt, nt=nt),
        grid=(B, nt),
        in_specs=[
            pl.BlockSpec((1, tt, C), cur(fw)),
            pl.BlockSpec((1, SUBLANES, C), prev(fw)),
            pl.BlockSpec((1, SUBLANES, C), nxt(fw)),
            pl.BlockSpec((1, tt, C), cur(bw)),
            pl.BlockSpec((1, SUBLANES, C), prev(bw)),
            pl.BlockSpec((1, SUBLANES, C), nxt(bw)),
            full((CONV_WIDTH, C)), full((1, C)),
            full(w_a_bf.shape), full((2, C)), full(w_x_bf.shape), full((2, C)), full((2, C)),
        ],
        out_specs=[
            pl.BlockSpec((1, tt, C), cur(fw)),
            pl.BlockSpec((1, tt, C), cur(bw)),
        ],
        out_shape=[jax.ShapeDtypeStruct((B, S, C), F32), jax.ShapeDtypeStruct((B, S, C), F32)],
        scratch_shapes=[
            pltpu.VMEM((tt, C), F32),
            pltpu.VMEM((tt, C), F32),
            pltpu.VMEM((1, C), F32),
            pltpu.VMEM((1, C), F32),
        ],
        compiler_params=_params(("parallel", "arbitrary")),
    )(z3, z3, z3, z3, z3, z3, conv_w, conv_b.reshape(1, C), w_a_bf, b_a, w_x_bf, b_x, lam)


def _out_proj_kernel(hf_ref, hb_ref, gr_ref, gu_ref, gv_ref, lng_ref, lnb_ref, ws_ref, bs_ref,
                     x_ref, w_ref, g_ref, h1_ref, n2_ref):
    ca = hf_ref.shape[-1]
    a_out = ((hf_ref[...] + hb_ref[...]) * gr_ref[...]).astype(BF16)
    v = gv_ref[...]
    mu = jnp.mean(v, axis=-1, keepdims=True)
    vc = v - mu
    var = jnp.mean(vc * vc, axis=-1, keepdims=True)
    vb = (vc * lax.rsqrt(var + EPS) * lng_ref[...] + lnb_ref[...]).astype(BF16)
    hd = v.shape[-1] // GMLP_HEADS
    b_rows = []
    for c in range(v.shape[0] // CHUNK):
        rows = slice(c * CHUNK, (c + 1) * CHUNK)
        mixed = jnp.concatenate(
            [jnp.dot(ws_ref[h], vb[rows, h * hd:(h + 1) * hd], preferred_element_type=F32)
             for h in range(GMLP_HEADS)], axis=1)
        b_rows.append((gu_ref[rows, :] * (mixed + bs_ref[...])).astype(BF16))
    b_out = jnp.concatenate(b_rows, axis=0)
    acc = jnp.dot(a_out, w_ref[0:ca, :], preferred_element_type=F32)
    acc = acc + jnp.dot(b_out, w_ref[ca:, :], preferred_element_type=F32)
    h1 = x_ref[...] + acc
    h1_ref[...] = h1
    ms = jnp.mean(h1 * h1, axis=-1, keepdims=True)
    n2_ref[...] = (h1 * lax.rsqrt(ms + EPS) * g_ref[...]).astype(BF16)


def _out_proj(hf, hb, z4, ln_g, ln_b, w_s_bf, bs_full, x2, w_bf, g, tm):
    M, D = x2.shape
    ca = hf.shape[-1]
    cb = ln_g.shape[-1]
    return pl.pallas_call(
        _out_proj_kernel,
        grid=(M // tm,),
        in_specs=[
            pl.BlockSpec((tm, ca), lambda i: (i, 0)),
            pl.BlockSpec((tm, ca), lambda i: (i, 0)),
            pl.BlockSpec((tm, ca), lambda i: (i, 1)),
            pl.BlockSpec((tm, cb), lambda i: (i, 2)),
            pl.BlockSpec((tm, cb), lambda i: (i, 3)),
            pl.BlockSpec((1, cb), lambda i: (0, 0)),
            pl.BlockSpec((1, cb), lambda i: (0, 0)),
            pl.BlockSpec(w_s_bf.shape, lambda i: (0, 0, 0)),
            pl.BlockSpec((CHUNK, cb), lambda i: (0, 0)),
            pl.BlockSpec((tm, D), lambda i: (i, 0)),
            pl.BlockSpec(w_bf.shape, lambda i: (0, 0)),
            pl.BlockSpec((1, D), lambda i: (0, 0)),
        ],
        out_specs=[pl.BlockSpec((tm, D), lambda i: (i, 0)), pl.BlockSpec((tm, D), lambda i: (i, 0))],
        out_shape=[jax.ShapeDtypeStruct((M, D), F32), jax.ShapeDtypeStruct((M, D), BF16)],
        compiler_params=_params(("parallel",)),
    )(hf, hb, z4, z4, z4, ln_g.reshape(1, cb), ln_b.reshape(1, cb), w_s_bf, bs_full, x2, w_bf,
      g.reshape(1, D))


ROUTE_T = LANES
PLANE = ROUTE_T + SUBLANES


def _batcher_pairs(n):
    pairs = []

    def merge(lo, m, r):
        step = r * 2
        if step < m:
            merge(lo, m, step)
            merge(lo + r, m, step)
            pairs.extend((i, i + r) for i in range(lo + r, lo + m - r, step))
        else:
            pairs.append((lo, lo + r))

    def sort(lo, m):
        if m > 1:
            sort(lo, m // 2)
            sort(lo + m // 2, m // 2)
            merge(lo, m, 1)

    sort(0, n)
    return pairs


def _beats(x, y):
    return (x[0] > y[0]) | ((x[0] == y[0]) & (x[1] < y[1]))


def _best(x, y):
    g = _beats(x, y)
    return (jnp.maximum(x[0], y[0]),) + tuple(jnp.where(g, p, q) for p, q in zip(x[1:], y[1:]))


def _compare_exchange(x, y):
    g = _beats(x, y)
    hi = (jnp.maximum(x[0], y[0]),) + tuple(jnp.where(g, p, q) for p, q in zip(x[1:], y[1:]))
    lo = (jnp.minimum(x[0], y[0]),) + tuple(jnp.where(g, q, p) for p, q in zip(x[1:], y[1:]))
    return hi, lo


def _sort_desc(items):
    items = list(items)
    for i, j in _batcher_pairs(len(items)):
        items[i], items[j] = _compare_exchange(items[i], items[j])
    return items


def _bitonic_merge(items):
    items = list(items)
    n = len(items)
    d = n // 2
    while d >= 1:
        for i in range(n):
            if i & d == 0:
                items[i], items[i + d] = _compare_exchange(items[i], items[i + d])
        d //= 2
    return items


def _merge_top(x, y):
    n = len(x)
    return _bitonic_merge([_best(x[k], y[n - 1 - k]) for k in range(n)])


def _merge_full(x, y):
    return _bitonic_merge(list(x) + list(reversed(y)))


def _route_kernel(n_ref, wq_ref, keys_ref, g_ref, q_scr, s_scr, sv_scr, si_scr, w_scr, i_scr, j_scr,
                  wt_scr, it_scr, jt_scr, gs_scr):
    T = ROUTE_T
    K = PEER_TOPK
    NK = PEER_KEYS
    q_scr[...] = jnp.dot(n_ref[...], wq_ref[...], preferred_element_type=F32).astype(BF16)

    H = PEER_HEADS
    row8 = lax.broadcasted_iota(jnp.int32, (SUBLANES, T), 0).astype(F32)

    for u in range(2 * H):
        s_scr[u * NK:(u + 1) * NK, :] = lax.dot_general(
            keys_ref[u], q_scr[:, u * NK:(u + 1) * NK], _NT, preferred_element_type=F32)

    def unit_body(u, _):
        c0 = pl.multiple_of(u * NK, NK)
        items = [(s_scr[pl.ds(c0 + b * SUBLANES, SUBLANES), :], row8 + float(b * SUBLANES))
                 for b in range(NK // SUBLANES)]
        items = _sort_desc(items)
        half = jnp.bitwise_and(u, 1)
        head = jnp.right_shift(u, 1)
        for k, (v, i) in enumerate(items):
            r0 = pl.multiple_of(((half * K + k) * H + head) * SUBLANES, SUBLANES)
            sv_scr[pl.ds(r0, SUBLANES), :] = v
            si_scr[pl.ds(r0, SUBLANES), :] = i
        return 0

    lax.fori_loop(0, 2 * H, unit_body, 0)

    def half_top(half):
        def residue_list(r):
            starts = [(half * K + k) * H * SUBLANES + r for k in range(K)]
            return (tuple(sv_scr[pl.ds(st, H, stride=SUBLANES), :] for st in starts),
                    tuple(si_scr[pl.ds(st, H, stride=SUBLANES), :] for st in starts))

        def merge_body(r, acc):
            new = residue_list(r)
            out = _merge_top(list(zip(*acc)), list(zip(*new)))
            return tuple(x[0] for x in out), tuple(x[1] for x in out)

        vals, idxs = lax.fori_loop(1, SUBLANES, merge_body, residue_list(0))
        return list(zip(vals, idxs))

    a = half_top(0)
    b = half_top(1)

    def pair(p, r):
        code = jnp.full((H, T), float(p * K + r), F32)
        return (a[p][0] + b[r][0], code, a[p][1] * float(NK) + b[r][1])

    rows = [[pair(p, r) for r in range(K // (p + 1))] for p in range(K)]
    singles = [rows[p][0] for p in range(8, K)]
    l1 = _merge_full(rows[1], singles)
    l2 = _sort_desc(rows[2] + rows[3] + rows[4] + rows[5] + rows[6])
    top = _merge_top(rows[0], l1)
    top = _merge_top(top, l2)
    top[K - 2] = _best(top[K - 2], rows[7][1])
    top[K - 1] = _best(top[K - 1], rows[7][0])
    top = _bitonic_merge(top)

    es = [jnp.exp(t[0] - top[0][0]) for t in top]
    z = es[0]
    for e in es[1:]:
        z = z + e
    inv = 0.5 / z
    for k in range(K):
        ident = top[k][2]
        ii = jnp.floor(ident * (1.0 / NK))
        w_scr[k * H:(k + 1) * H, :] = es[k] * inv
        i_scr[k * H:(k + 1) * H, :] = ii
        j_scr[k * H:(k + 1) * H, :] = ident - ii * float(NK)

    wt_scr[...] = w_scr[...].T
    it_scr[...] = i_scr[...].T
    jt_scr[...] = j_scr[...].T

    sub = lax.broadcasted_iota(jnp.int32, (NK, PEER_HEADS * K), 0).astype(F32)

    zero_blk = jnp.zeros((PEER_HEADS * K, NK), BF16)

    def factors(t):
        wrow = wt_scr[pl.ds(t, 1), :]
        irow = it_scr[pl.ds(t, 1), :]
        jrow = jt_scr[pl.ds(t, 1), :]
        p1 = jnp.where(sub == irow, wrow, 0.0).astype(BF16)
        p2 = jnp.where(sub == jrow, 1.0, 0.0).astype(BF16).T
        return p1, p2

    def pair_body(q, _):
        t0 = 2 * q
        p1a, p2a = factors(t0)
        p1b, p2b = factors(t0 + 1)
        lhs = jnp.concatenate([p1a, p1b], axis=1)
        rhs = jnp.concatenate([jnp.concatenate([p2a, zero_blk], axis=1),
                               jnp.concatenate([zero_blk, p2b], axis=1)], axis=0)
        g = jnp.dot(lhs, rhs, preferred_element_type=F32)
        for blk in range(NK // SUBLANES):
            rows = slice(blk * SUBLANES, (blk + 1) * SUBLANES)
            base = blk * SUBLANES * PLANE + t0
            gs_scr[pl.ds(base, SUBLANES, stride=PLANE), :] = g[rows, 0:NK]
            gs_scr[pl.ds(base + 1, SUBLANES, stride=PLANE), :] = g[rows, NK:2 * NK]
        return 0

    lax.fori_loop(0, T // 2, pair_body, 0, unroll=True)

    for i in range(NK):
        g_ref[:, i * NK:(i + 1) * NK] = gs_scr[i * PLANE:i * PLANE + T, :].astype(BF16)


def _route(n2, wq_bf, keys_bf):
    M, D = n2.shape
    Q = wq_bf.shape[1]
    T = ROUTE_T
    NK = PEER_KEYS
    HK = PEER_HEADS * PEER_TOPK
    return pl.pallas_call(
        _route_kernel,
        grid=(M // T,),
        in_specs=[
            pl.BlockSpec((T, D), lambda i: (i, 0)),
            pl.BlockSpec((D, Q), lambda i: (0, 0)),
            pl.BlockSpec(keys_bf.shape, lambda i: (0, 0, 0)),
        ],
        out_specs=pl.BlockSpec((T, NK * NK), lambda i: (i, 0)),
        out_shape=jax.ShapeDtypeStruct((M, NK * NK), BF16),
        scratch_shapes=[
            pltpu.VMEM((T, Q), BF16),
            pltpu.VMEM((2 * PEER_HEADS * NK, T), F32),
            pltpu.VMEM((2 * HK * SUBLANES, T), F32), pltpu.VMEM((2 * HK * SUBLANES, T), F32),
            pltpu.VMEM((HK, T), F32), pltpu.VMEM((HK, T), F32), pltpu.VMEM((HK, T), F32),
            pltpu.VMEM((T, HK), F32), pltpu.VMEM((T, HK), F32), pltpu.VMEM((T, HK), F32),
            pltpu.VMEM((NK * PLANE, NK), F32),
        ],
        compiler_params=_params(("parallel",)),
    )(n2, wq_bf, keys_bf)


def _expert_kernel(n_ref, u_ref, v_ref, g_ref, h_ref, fg_ref, o_ref, *, sub, final_norm):
    k = pl.program_id(1)

    @pl.when(k == 0)
    def _():
        o_ref[...] = jnp.zeros_like(o_ref)

    n = n_ref[...]
    te = u_ref.shape[0]
    acc = None
    for c in range(te // sub):
        cs = slice(c * sub, (c + 1) * sub)
        act = lax.dot_general(n, u_ref[cs, :], _NT, preferred_element_type=F32)
        t = jnp.tanh(_gelu_tanh_arg(act).astype(BF16))
        a = act.astype(BF16) * (1.0 + t) * g_ref[:, cs]
        part = jnp.dot(a, v_ref[cs, :], preferred_element_type=F32)
        acc = part if acc is None else acc + part
    o_ref[...] += acc
    hs = h_ref.shape[0]
    r0 = pl.multiple_of(k * hs, hs)
    o_ref[pl.ds(r0, hs), :] += h_ref[...]

    if final_norm:
        @pl.when(k == pl.num_programs(1) - 1)
        def _():
            h2 = o_ref[...]
            ms = jnp.mean(h2 * h2, axis=-1, keepdims=True)
            o_ref[...] = h2 * lax.rsqrt(ms + EPS) * fg_ref[...]


def _experts(n2, u_bf, v_bf, gates, h1, fg, tt, te, sub, final_norm):
    M, D = n2.shape
    nk = u_bf.shape[0] // te
    hs = tt // nk
    assert hs * nk == tt and hs % SUBLANES == 0
    return pl.pallas_call(
        functools.partial(_expert_kernel, sub=sub, final_norm=final_norm),
        grid=(M // tt, nk),
        in_specs=[
            pl.BlockSpec((tt, D), lambda i, k: (i, 0)),
            pl.BlockSpec((te, D), lambda i, k: (k, 0)),
            pl.BlockSpec((te, D), lambda i, k: (k, 0)),
            pl.BlockSpec((tt, te), lambda i, k: (i, k)),
            pl.BlockSpec((hs, D), lambda i, k: (i * nk + k, 0)),
            pl.BlockSpec((1, D), lambda i, k: (0, 0)),
        ],
        out_specs=pl.BlockSpec((tt, D), lambda i, k: (i, 0)),
        out_shape=jax.ShapeDtypeStruct((M, D), F32),
        compiler_params=_params(("parallel", "arbitrary"), EXPERTS_VMEM_LIMIT),
    )(n2, u_bf, v_bf, gates, h1, fg.reshape(1, D))


def _tile(n, want):
    t = min(n, want)
    assert n % t == 0, (n, t)
    return t


class _Tiles(NamedTuple):
    mix_in_rows: int
    lru_steps: int
    out_proj_rows: int
    expert_tokens: int
    expert_tile: int
    expert_chunk: int


def _plan(M, S, E):
    expert_tile = _tile(E, 1024)
    return _Tiles(_tile(M, 1024), _tile(S, 512), _tile(M, 512), _tile(M, 1024), expert_tile,
                  _tile(expert_tile, 512))


def kernel(x, mix_norm_g, w_in, conv_w, conv_b, lru_w_a, lru_b_a, lru_w_x, lru_b_x, lru_lambda,
           gmlp_ln_g, gmlp_ln_b, gmlp_w_s, gmlp_b_s, w_out, ffn_norm_g, peer_w_q, peer_sub_keys,
           peer_u, peer_v, final_norm_g):
    B, S, D = x.shape
    M = B * S
    depth = w_in.shape[0]
    assert S % CHUNK == 0 and M % ROUTE_T == 0
    t = _plan(M, S, peer_u.shape[1])
    h = x.reshape(M, D)
    for l in range(depth):
        z4 = _mix_in(h, mix_norm_g[l], w_in[l].astype(BF16), t.mix_in_rows)
        cw = z4.shape[1] // 4
        hf, hb = _lru(z4.reshape(B, S, 4 * cw), conv_w[l], conv_b[l],
                      (0.5 * lru_w_a[l]).astype(BF16), 0.5 * lru_b_a[l],
                      (0.5 * lru_w_x[l]).astype(BF16), 0.5 * lru_b_x[l],
                      lru_lambda[l], t.lru_steps)
        bs_full = jnp.repeat(gmlp_b_s[l].T, cw // GMLP_HEADS, axis=1)
        h1, n2 = _out_proj(hf.reshape(M, cw), hb.reshape(M, cw), z4, gmlp_ln_g[l], gmlp_ln_b[l],
                           gmlp_w_s[l].astype(BF16), bs_full, h,
                           w_out[l].astype(BF16), ffn_norm_g[l], t.out_proj_rows)
        keys = peer_sub_keys[l].reshape(PEER_HEADS * 2, PEER_KEYS, -1).astype(BF16)
        gates = _route(n2, peer_w_q[l].astype(BF16), keys)
        h = _experts(n2, peer_u[l].astype(BF16), peer_v[l].astype(BF16), gates, h1, final_norm_g,
                     t.expert_tokens, t.expert_tile, t.expert_chunk, l == depth - 1)
    return h.reshape(B, S, D)
```
